```python
import jax, jax.numpy as jnp
from jax import lax
import numpy as np

D_MODEL = 1024
BATCH = 8
SEQ = 4096
DEPTH = 1

PLE_DIM = 256
HG_HEADS = 4
HG_DK = 128
HG_DV = 128
HG_KW = HG_HEADS * HG_DK
HG_VW = HG_HEADS * HG_DV
HG_CHUNK = 64
FOX_HEADS = 8
FOX_DH = 64
FOX_W = FOX_HEADS * FOX_DH
FOX_BLOCK = 128
D_FF = ((-(-8 * D_MODEL // 3) + 255) // 256) * 256
EPS = 1e-6

OFF_HG_Q = 0
OFF_HG_F = OFF_HG_Q + HG_KW
OFF_HG_I = OFF_HG_F + HG_KW
OFF_HG_G = OFF_HG_I + HG_VW
OFF_FOX_Q = OFF_HG_G + HG_VW
OFF_FOX_K = OFF_FOX_Q + FOX_W
OFF_FOX_V = OFF_FOX_K + FOX_W
OFF_FOX_F = OFF_FOX_V + FOX_W
OFF_GATE = OFF_FOX_F + FOX_HEADS
IN_COLS = OFF_GATE + 2 * D_MODEL

kernel_name = "hgrn2_fox_gated_hybrid"


def rms_norm(x, g):
    xf = x.astype(jnp.float32)
    y = xf * lax.rsqrt(jnp.mean(xf * xf, axis=-1, keepdims=True) + EPS)
    return (y * g.astype(jnp.float32)).astype(x.dtype)


def hgrn2_mixer(q, f_logit, i, g, lb, o_gain):
    B, S, _ = q.shape
    f32 = jnp.float32
    qf = jax.nn.silu(q.astype(f32))
    fg = lb + (1.0 - lb) * jax.nn.sigmoid(f_logit.astype(f32))
    kf = 1.0 - fg
    logf = jnp.log(fg)
    vf = i.astype(f32)
    n = S // HG_CHUNK

    def to_chunks(t, d):
        return t.reshape(B, n, HG_CHUNK, HG_HEADS, d).transpose(1, 0, 3, 2, 4)

    qc = to_chunks(qf, HG_DK)
    kc = to_chunks(kf, HG_DK)
    vc = to_chunks(vf, HG_DV)
    bc = jnp.cumsum(to_chunks(logf, HG_DK), axis=3)
    causal = jnp.tril(jnp.ones((HG_CHUNK, HG_CHUNK), dtype=bool))[:, :, None]

    def step(state, xs):
        qt, kt, vt, bt = xs
        inter = jnp.einsum('bhtc,bhcv->bhtv', qt * jnp.exp(bt), state)
        diff = bt[:, :, :, None, :] - bt[:, :, None, :, :]
        decay = jnp.exp(jnp.where(causal, diff, -jnp.inf))
        scores = jnp.einsum('bhtc,bhsc,bhtsc->bhts', qt, kt, decay)
        intra = jnp.einsum('bhts,bhsv->bhtv', scores, vt)
        b_last = bt[:, :, -1:, :]
        new_state = (jnp.exp(b_last[:, :, 0, :])[..., None] * state
                     + jnp.einsum('bhsc,bhsv->bhcv', kt * jnp.exp(b_last - bt), vt))
        return new_state, inter + intra

    s0 = jnp.zeros((B, HG_HEADS, HG_DK, HG_DV), f32)
    _, o = lax.scan(step, s0, (qc, kc, vc, bc))
    o = o.transpose(1, 0, 3, 2, 4).reshape(B, S, HG_HEADS, HG_DV)
    o = rms_norm(o, o_gain).reshape(B, S, HG_VW)
    o = o * jax.nn.silu(g.astype(f32))
    return o.astype(q.dtype)


def forgetting_attention(q, k, v, f_logit, f_bias, q_gain, k_gain):
    B, S, _ = q.shape
    f32 = jnp.float32

    def heads(t):
        return t.reshape(B, S, FOX_HEADS, FOX_DH).transpose(0, 2, 1, 3)

    qh = rms_norm(heads(q), q_gain).astype(f32)
    kh = rms_norm(heads(k), k_gain).astype(f32)
    vh = heads(v)
    logf = jax.nn.log_sigmoid(f_logit.astype(f32) + f_bias.astype(f32))
    c = jnp.cumsum(logf, axis=1).transpose(0, 2, 1)
    scale = FOX_DH ** -0.5
    outs = []
    for blk in range(S // FOX_BLOCK):
        t0 = blk * FOX_BLOCK
        t1 = t0 + FOX_BLOCK
        s = jnp.einsum('bhtd,bhsd->bhts', qh[:, :, t0:t1], kh[:, :, :t1]) * scale
        s = s + c[:, :, t0:t1, None] - c[:, :, None, :t1]
        mask = (t0 + jnp.arange(FOX_BLOCK))[:, None] >= jnp.arange(t1)[None, :]
        s = jnp.where(mask, s, -jnp.inf)
        pr = jax.nn.softmax(s, axis=-1).astype(vh.dtype)
        outs.append(jnp.einsum('bhts,bhsd->bhtd', pr, vh[:, :, :t1]))
    o = jnp.concatenate(outs, axis=2)
    return o.transpose(0, 2, 1, 3).reshape(B, S, FOX_W)


def setup_inputs(seed: int = 0) -> dict:
    key = jax.random.key(seed)
    ks = jax.random.split(key, 24)
    f32 = jnp.float32

    def w(k, shape, fan_in):
        return jax.random.normal(k, shape, f32) * (fan_in ** -0.5)

    def gain(k, shape):
        return 1.0 + 0.02 * jax.random.normal(k, shape, f32)

    return {
        "x": jax.random.normal(ks[0], (BATCH, SEQ, D_MODEL), f32),
        "p": jax.random.normal(ks[1], (DEPTH, BATCH, SEQ, PLE_DIM), f32),
        "norm_mix_g": gain(ks[2], (DEPTH, D_MODEL)),
        "w_in": w(ks[3], (DEPTH, D_MODEL, IN_COLS), D_MODEL),
        "hg_lb_logits": 0.1 * jax.random.normal(ks[4], (DEPTH + 1, HG_KW), f32),
        "hg_onorm_g": gain(ks[5], (DEPTH, HG_DV)),
        "fox_f_bias": jax.random.uniform(ks[6], (DEPTH, FOX_HEADS), f32, minval=1.0, maxval=4.0),
        "fox_q_norm_g": gain(ks[7], (DEPTH, FOX_DH)),
        "fox_k_norm_g": gain(ks[8], (DEPTH, FOX_DH)),
        "w_branch_a": w(ks[9], (DEPTH, HG_VW, D_MODEL), HG_VW),
        "w_branch_b": w(ks[10], (DEPTH, FOX_W, D_MODEL), FOX_W),
        "w_out": w(ks[11], (DEPTH, D_MODEL, D_MODEL), D_MODEL),
        "norm_ffn_g": gain(ks[12], (DEPTH, D_MODEL)),
        "w_ffn_gate": w(ks[13], (DEPTH, D_MODEL, D_FF), D_MODEL),
        "w_ffn_up": w(ks[14], (DEPTH, D_MODEL, D_FF), D_MODEL),
        "w_ffn_down": w(ks[15], (DEPTH, D_FF, D_MODEL), D_FF),
        "norm_ple_g": gain(ks[16], (DEPTH, D_MODEL)),
        "w_ple_gate": w(ks[17], (DEPTH, D_MODEL, D_MODEL), D_MODEL),
        "w_ple_proj": w(ks[18], (DEPTH, PLE_DIM, D_MODEL), PLE_DIM),
    }


def reference(x, p, norm_mix_g, w_in, hg_lb_logits, hg_onorm_g, fox_f_bias, fox_q_norm_g,
              fox_k_norm_g, w_branch_a, w_branch_b, w_out, norm_ffn_g, w_ffn_gate, w_ffn_up,
              w_ffn_down, norm_ple_g, w_ple_gate, w_ple_proj):
    lower_bounds = jnp.cumsum(jax.nn.softmax(hg_lb_logits.astype(jnp.float32), axis=0), axis=0)
    for layer in range(DEPTH):
        h = rms_norm(x, norm_mix_g[layer])
        z = h @ w_in[layer]
        y_a = hgrn2_mixer(z[..., OFF_HG_Q:OFF_HG_F], z[..., OFF_HG_F:OFF_HG_I],
                          z[..., OFF_HG_I:OFF_HG_G], z[..., OFF_HG_G:OFF_FOX_Q],
                          lower_bounds[layer], hg_onorm_g[layer])
        y_b = forgetting_attention(z[..., OFF_FOX_Q:OFF_FOX_K], z[..., OFF_FOX_K:OFF_FOX_V],
                                   z[..., OFF_FOX_V:OFF_FOX_F], z[..., OFF_FOX_F:OFF_GATE],
                                   fox_f_bias[layer], fox_q_norm_g[layer], fox_k_norm_g[layer])
        gate_a = jax.nn.sigmoid(z[..., OFF_GATE:OFF_GATE + D_MODEL])
        gate_b = jax.nn.sigmoid(z[..., OFF_GATE + D_MODEL:IN_COLS])
        merged = gate_a * (y_a @ w_branch_a[layer]) + gate_b * (y_b @ w_branch_b[layer])
        x = x + merged @ w_out[layer]
        hf = rms_norm(x, norm_ffn_g[layer])
        x = x + (jax.nn.silu(hf @ w_ffn_gate[layer]) * (hf @ w_ffn_up[layer])) @ w_ffn_down[layer]
        hp = rms_norm(x, norm_ple_g[layer])
        x = x + jax.nn.sigmoid(hp @ w_ple_gate[layer]) * (p[layer] @ w_ple_proj[layer])
    return x
```

```python
import functools

import numpy as np
import jax
import jax.numpy as jnp
from jax import lax
from jax.experimental import pallas as pl
from jax.experimental.pallas import tpu as pltpu

F32 = jnp.float32
BF16 = jnp.bfloat16
EPS = 1e-6

HG_HEADS = 4
HG_DK = 128
HG_DV = 128
HG_W = HG_HEADS * HG_DK
HG_CHUNK = 64
FOX_HEADS = 8
FOX_DH = 64
FOX_W = FOX_HEADS * FOX_DH
LANES = 128
VMEM_LIMIT_BYTES = 56 * 1024 * 1024


def _dot(a, b):
    return jnp.dot(a, b, preferred_element_type=F32)


def _dot_nt(a, b):
    return lax.dot_general(a, b, (((1,), (1,)), ((), ())), preferred_element_type=F32)


def _dot_tn(a, b):
    return lax.dot_general(a, b, (((0,), (0,)), ((), ())), preferred_element_type=F32)


def _split3(a):
    hi = a.astype(BF16)
    r1 = a - hi.astype(F32)
    mid = r1.astype(BF16)
    lo = (r1 - mid.astype(F32)).astype(BF16)
    return hi, mid, lo


def _dot_exact_rhs(m01, a):
    hi, mid, lo = _split3(a)
    return _dot(m01, hi) + _dot(m01, mid) + _dot(m01, lo)


def _dot_exact_lhs(a, m01):
    hi, mid, lo = _split3(a)
    return _dot(hi, m01) + _dot(mid, m01) + _dot(lo, m01)


def _sigmoid(x):
    return 1.0 / (1.0 + jnp.exp(-x))


def _silu(x):
    return x * _sigmoid(x)


def _rms_scale(x):
    return lax.rsqrt(jnp.mean(x * x, axis=-1, keepdims=True) + EPS)


_IN_SEGS = (
    ("hg_q", HG_W, BF16),
    ("hg_f", HG_W, F32),
    ("hg_i", HG_W, BF16),
    ("hg_g", HG_W, BF16),
    ("fox_q", FOX_W, BF16),
    ("fox_k", FOX_W, BF16),
    ("fox_v", FOX_W, BF16),
    ("gate_a", 1024, BF16),
    ("gate_b", 1024, BF16),
)


def _in_proj_kernel(x_ref, g_ref, w_ref, wf_ref, *out_refs):
    x = x_ref[...]
    h = (x * _rms_scale(x) * g_ref[...]).astype(BF16)
    off = 0
    for (_, width, dtype), o_ref in zip(_IN_SEGS, out_refs[:-1]):
        o_ref[...] = _dot(h, w_ref[:, off:off + width]).astype(dtype)
        off += width
    out_refs[-1][0] = _dot_nt(wf_ref[...], h)


def _in_proj(x2d, g, w_main, w_ff, batch, seq, tm):
    tokens, d = x2d.shape
    n_main = w_main.shape[1]
    per_b = seq // tm
    out_shape = [jax.ShapeDtypeStruct((tokens, wd), dt) for _, wd, dt in _IN_SEGS]
    out_shape.append(jax.ShapeDtypeStruct((batch, FOX_HEADS, seq), F32))
    out_specs = [pl.BlockSpec((tm, wd), lambda i: (i, 0)) for _, wd, _ in _IN_SEGS]
    out_specs.append(pl.BlockSpec((1, FOX_HEADS, tm), lambda i: (i // per_b, 0, i % per_b)))
    return pl.pallas_call(
        _in_proj_kernel,
        grid=(tokens // tm,),
        in_specs=[
            pl.BlockSpec((tm, d), lambda i: (i, 0)),
            pl.BlockSpec((1, d), lambda i: (0, 0)),
            pl.BlockSpec((d, n_main), lambda i: (0, 0), pipeline_mode=pl.Buffered(1)),
            pl.BlockSpec((FOX_HEADS, d), lambda i: (0, 0)),
        ],
        out_specs=out_specs,
        out_shape=out_shape,
        compiler_params=pltpu.CompilerParams(
            dimension_semantics=("parallel",), vmem_limit_bytes=VMEM_LIMIT_BYTES),
        name="in_proj",
    )(x2d, g, w_main, w_ff)


def _fox_decay_kernel(f_ref, bias_ref, c_ref):
    nh, seq = f_ref.shape[1], f_ref.shape[2]
    nblk = seq // LANES
    rows = nblk * nh
    bias = bias_ref[...]
    pieces = []
    for j in range(nblk):
        z = f_ref[0, :, j * LANES:(j + 1) * LANES] + bias
        pieces.append(jnp.minimum(z, 0.0) - jnp.log1p(jnp.exp(-jnp.abs(z))))
    ls = jnp.concatenate(pieces, axis=0)
    r = lax.broadcasted_iota(jnp.int32, (LANES, LANES), 0)
    c = lax.broadcasted_iota(jnp.int32, (LANES, LANES), 1)
    upper = (r <= c).astype(BF16)
    local = _dot_exact_lhs(ls, upper)
    total = jnp.broadcast_to(local[:, LANES - 1:LANES], (rows, LANES))
    rr = lax.broadcasted_iota(jnp.int32, (rows, rows), 0)
    cc = lax.broadcasted_iota(jnp.int32, (rows, rows), 1)
    prev_blocks = ((rr % nh == cc % nh) & (cc // nh < rr // nh)).astype(BF16)
    out = local + _dot_exact_rhs(prev_blocks, total)
    for j in range(nblk):
        c_ref[0, :, j * LANES:(j + 1) * LANES] = out[j * nh:(j + 1) * nh, :]


def _fox_decay(f_logits, bias_lanes):
    batch, nh, seq = f_logits.shape
    return pl.pallas_call(
        _fox_decay_kernel,
        grid=(batch,),
        in_specs=[
            pl.BlockSpec((1, nh, seq), lambda b: (b, 0, 0)),
            pl.BlockSpec((nh, LANES), lambda b: (0, 0)),
        ],
        out_specs=pl.BlockSpec((1, nh, seq), lambda b: (b, 0, 0)),
        out_shape=jax.ShapeDtypeStruct((batch, nh, seq), F32),
        compiler_params=pltpu.CompilerParams(dimension_semantics=("parallel",)),
        name="fox_decay",
    )(f_logits, bias_lanes)


_HG_LEVELS = (32, 16, 8, 4, 2, 1)


def _hgrn2_decay_matrix():
    c = HG_CHUNK
    t = np.arange(c)[:, None]
    j = np.arange(c)[None, :]
    groups = [(j <= t), (j > t)]
    for n in _HG_LEVELS:
        ref = (t // (2 * n)) * (2 * n) + n - 1
        second = (t & n) != 0
        groups.append(np.where(second, (j > ref) & (j <= t), (j > t) & (j <= ref)))
    return np.concatenate(groups, axis=0).astype(np.float32)


def _hgrn2_kernel(q_ref, f_ref, i_ref, g_ref, lb_ref, og_ref, em_ref, y_ref, st_ref, *, layer):
    ts = q_ref.shape[0]
    c = HG_CHUNK

    @pl.when(pl.program_id(1) == 0)
    def _():
        st_ref[...] = jnp.zeros_like(st_ref)

    lbl = lb_ref[...]
    lbe = jnp.exp(lbl - jnp.max(lbl, axis=0, keepdims=True))
    lb = jnp.sum(lbe[0:layer + 1, :], axis=0, keepdims=True) / jnp.sum(lbe, axis=0, keepdims=True)

    em = em_ref[...]
    og = og_ref[...]
    row = lax.broadcasted_iota(jnp.int32, (c, HG_W), 0)
    tt = lax.broadcasted_iota(jnp.int32, (c, c), 0)
    ss = lax.broadcasted_iota(jnp.int32, (c, c), 1)
    txs = tt ^ ss
    causal = tt >= ss

    for ck in range(ts // c):
        rows = slice(ck * c, (ck + 1) * c)
        qf = _silu(q_ref[rows, :].astype(F32))
        fg = lb + (1.0 - lb) * _sigmoid(f_ref[rows, :])
        kf = 1.0 - fg
        logf = jnp.log(fg)
        decay = jnp.exp(_dot_exact_rhs(em, logf))
        v = i_ref[rows, :]
        gate = _silu(g_ref[rows, :].astype(F32))
        qe = (qf * decay[0:c]).astype(BF16)
        kd = (kf * decay[c:2 * c]).astype(BF16)
        e_last = decay[c - 1:c, :]
        mixed = [(qf.astype(BF16), kf.astype(BF16))]
        for li, n in enumerate(_HG_LEVELS):
            u = (jnp.where((row & n) != 0, qf, kf) * decay[(2 + li) * c:(3 + li) * c]).astype(BF16)
            mixed.append((u, u))
        for h in range(HG_HEADS):
            cols = slice(h * HG_DK, (h + 1) * HG_DK)
            qd, kdiag = mixed[0]
            scores = _dot_nt(qd[:, cols], kdiag[:, cols])
            for li, n in enumerate(reversed(_HG_LEVELS)):
                u = mixed[len(_HG_LEVELS) - li][0][:, cols]
                scores = jnp.where(txs >= n, _dot_nt(u, u), scores)
            scores = jnp.where(causal, scores, 0.0)
            st = st_ref[h]
            o = _dot_nt(qe[:, cols], st.astype(BF16)) + _dot(scores.astype(BF16), v[:, cols])
            st_ref[h] = st * e_last[:, cols] + _dot_tn(v[:, cols], kd[:, cols])
            o = o * _rms_scale(o) * og
            y_ref[rows, cols] = (o * gate[:, cols]).astype(y_ref.dtype)


def _hgrn2(q, f, i, g, lb_logits, o_gain, layer, batch, seq, ts):
    tokens = q.shape[0]
    per_b = seq // ts
    em = jnp.asarray(_hgrn2_decay_matrix(), dtype=BF16)
    tile = pl.BlockSpec((ts, HG_W), lambda b, s: (b * per_b + s, 0))
    return pl.pallas_call(
        functools.partial(_hgrn2_kernel, layer=layer),
        grid=(batch, per_b),
        in_specs=[
            tile, tile, tile, tile,
            pl.BlockSpec(lb_logits.shape, lambda b, s: (0, 0)),
            pl.BlockSpec((1, HG_DV), lambda b, s: (0, 0)),
            pl.BlockSpec(em.shape, lambda b, s: (0, 0)),
        ],
        out_specs=tile,
        out_shape=jax.ShapeDtypeStruct((tokens, HG_W), BF16),
        scratch_shapes=[pltpu.VMEM((HG_HEADS, HG_DV, HG_DK), F32)],
        compiler_params=pltpu.CompilerParams(dimension_semantics=("parallel", "arbitrary")),
        name="hgrn2",
    )(q, f, i, g, lb_logits, o_gain, em)


def _pair_norm(x, gain):
    r = lax.broadcasted_iota(jnp.int32, (LANES, LANES), 0)
    c = lax.broadcasted_iota(jnp.int32, (LANES, LANES), 1)
    same_head = (r // FOX_DH == c // FOX_DH).astype(BF16)
    sq = x * x
    hi = sq.astype(BF16)
    lo = (sq - hi.astype(F32)).astype(BF16)
    ssum = _dot(hi, same_head) + _dot(lo, same_head)
    return x * lax.rsqrt(ssum * (1.0 / FOX_DH) + EPS) * gain


def _fox_attn_kernel(q_ref, k_ref, v_ref, c_ref, qg_ref, kg_ref, o_ref, kn_ref, *, tq, tk):
    seq = k_ref.shape[1]
    i = pl.program_id(2)

    @pl.when(i == 0)
    def _():
        kg = kg_ref[...]
        for r0 in range(0, seq, 512):
            rows = slice(r0, min(r0 + 512, seq))
            kn_ref[rows, :] = _pair_norm(k_ref[0, rows, :].astype(F32), kg).astype(BF16)

    qn = _pair_norm(q_ref[0].astype(F32), qg_ref[...]) * (FOX_DH ** -0.5)
    lane = lax.broadcasted_iota(jnp.int32, (tq, LANES), 1)
    first = lane < FOX_DH
    rr = lax.broadcasted_iota(jnp.int32, (tq, tk), 0)
    cc = lax.broadcasted_iota(jnp.int32, (tq, tk), 1)

    def head(hh):
        qh = jnp.where(first if hh == 0 else ~first, qn, 0.0).astype(BF16)

        def step(kb, carry, masked):
            m, l, acc = carry
            k0 = pl.multiple_of(kb * tk, tk)
            s = _dot_nt(qh, kn_ref[pl.ds(k0, tk), :])
            s = s - c_ref[0, 0, hh:hh + 1, pl.ds(k0, tk)]
            if masked:
                s = jnp.where(rr >= cc, s, -jnp.inf)
            m_new = jnp.maximum(m, jnp.max(s, axis=-1, keepdims=True))
            alpha = jnp.exp(m - m_new)
            p = jnp.exp(s - m_new)
            l = alpha * l + jnp.sum(p, axis=-1, keepdims=True)
            acc = alpha * acc + _dot(p.astype(BF16), v_ref[0, pl.ds(k0, tk), :])
            return m_new, l, acc

        init = (jnp.full((tq, 1), -jnp.inf, F32), jnp.zeros((tq, 1), F32),
                jnp.zeros((tq, LANES), F32))
        carry = lax.fori_loop(0, i, lambda kb, cr: step(kb, cr, False), init)
        _, l, acc = step(i, carry, True)
        return acc / l

    o_ref[0] = jnp.where(first, head(0), head(1)).astype(o_ref.dtype)


def _fox_attn(q, k, v, c, q_gain2, k_gain2, tq):
    batch, seq, _ = q.shape
    pairs = FOX_HEADS // 2
    c4 = c.reshape(batch, pairs, 2, seq)
    kernel = functools.partial(_fox_attn_kernel, tq=tq, tk=tq)
    return pl.pallas_call(
        kernel,
        grid=(batch, pairs, seq // tq),
        in_specs=[
            pl.BlockSpec((1, tq, LANES), lambda b, p, i: (b, i, p)),
            pl.BlockSpec((1, seq, LANES), lambda b, p, i: (b, 0, p)),
            pl.BlockSpec((1, seq, LANES), lambda b, p, i: (b, 0, p)),
            pl.BlockSpec((1, 1, 2, seq), lambda b, p, i: (b, p, 0, 0)),
            pl.BlockSpec((1, LANES), lambda b, p, i: (0, 0)),
            pl.BlockSpec((1, LANES), lambda b, p, i: (0, 0)),
        ],
        out_specs=pl.BlockSpec((1, tq, LANES), lambda b, p, i: (b, i, p)),
        out_shape=jax.ShapeDtypeStruct((batch, seq, FOX_W), BF16),
        scratch_shapes=[pltpu.VMEM((seq, LANES), BF16)],
        compiler_params=pltpu.CompilerParams(
            dimension_semantics=("parallel", "parallel", "arbitrary")),
        name="fox_attn",
    )(q, k, v, c4, q_gain2, k_gain2)


def _ff_chunks(d_ff):
    chunks, off = [], 0
    while off < d_ff:
        wd = min(1024, d_ff - off)
        chunks.append((off, wd))
        off += wd
    return chunks


def _tail_kernel(x_ref, ya_ref, yb_ref, ga_ref, gb_ref, p_ref, wa_ref, wb_ref, wo_ref,
                 gf_ref, wg_ref, wu_ref, wd_ref, gp_ref, wpg_ref, wpp_ref, o_ref):
    ga = _sigmoid(ga_ref[...].astype(F32))
    gb = _sigmoid(gb_ref[...].astype(F32))
    merged = ga * _dot(ya_ref[...], wa_ref[...]) + gb * _dot(yb_ref[...], wb_ref[...])
    x = x_ref[...] + _dot(merged.astype(BF16), wo_ref[...])

    hf = (x * _rms_scale(x) * gf_ref[...]).astype(BF16)
    ffn = None
    for off, wd in _ff_chunks(wg_ref.shape[1]):
        act = _silu(_dot(hf, wg_ref[:, off:off + wd])) * _dot(hf, wu_ref[:, off:off + wd])
        part = _dot(act.astype(BF16), wd_ref[off:off + wd, :])
        ffn = part if ffn is None else ffn + part
    x = x + ffn

    hp = (x * _rms_scale(x) * gp_ref[...]).astype(BF16)
    emb = _dot(p_ref[...].astype(BF16), wpp_ref[...])
    o_ref[...] = x + _sigmoid(_dot(hp, wpg_ref[...])) * emb


def _tail(x2d, ya, yb, ga, gb, p2d, wa, wb, wo, gf, wg, wu, wd, gp, wpg, wpp, tm):
    tokens, d = x2d.shape

    def tile(width):
        return pl.BlockSpec((tm, width), lambda i: (i, 0))

    def whole(arr):
        return pl.BlockSpec(arr.shape, lambda i: (0, 0), pipeline_mode=pl.Buffered(1))

    return pl.pallas_call(
        _tail_kernel,
        grid=(tokens // tm,),
        in_specs=[
            tile(d), tile(ya.shape[1]), tile(yb.shape[1]), tile(d), tile(d), tile(p2d.shape[1]),
            whole(wa), whole(wb), whole(wo), whole(gf), whole(wg), whole(wu), whole(wd),
            whole(gp), whole(wpg), whole(wpp),
        ],
        out_specs=tile(d),
        out_shape=jax.ShapeDtypeStruct((tokens, d), F32),
        compiler_params=pltpu.CompilerParams(
            dimension_semantics=("parallel",), vmem_limit_bytes=VMEM_LIMIT_BYTES),
        name="tail",
    )(x2d, ya, yb, ga, gb, p2d, wa, wb, wo, gf, wg, wu, wd, gp, wpg, wpp)


def _tile(n, pref):
    return pref if n % pref == 0 else n


def kernel(x, p, norm_mix_g, w_in, hg_lb_logits, hg_onorm_g, fox_f_bias, fox_q_norm_g,
           fox_k_norm_g, w_branch_a, w_branch_b, w_out, norm_ffn_g, w_ffn_gate, w_ffn_up,
           w_ffn_down, norm_ple_g, w_ple_gate, w_ple_proj):
    batch, seq, d = x.shape
    depth = p.shape[0]
    off_fox_f = 4 * HG_W + 3 * FOX_W
    off_gate = off_fox_f + FOX_HEADS
    assert w_in.shape[2] == off_gate + 2 * d and d == 1024
    assert seq % HG_CHUNK == 0 and seq % LANES == 0

    tm = _tile(seq, 512)
    ts = _tile(seq, 128)
    tq = _tile(seq, 256)

    x2d = x.reshape(batch * seq, d)
    for layer in range(depth):
        wl = w_in[layer]
        w_main = jnp.concatenate([wl[:, :off_fox_f], wl[:, off_gate:]], axis=1).astype(BF16)
        w_ff = wl[:, off_fox_f:off_gate].T.astype(BF16)
        (hg_q, hg_f, hg_i, hg_g, fox_q, fox_k, fox_v, gate_a, gate_b, fox_f) = _in_proj(
            x2d, norm_mix_g[layer][None, :], w_main, w_ff, batch, seq, tm)

        bias_lanes = jnp.broadcast_to(fox_f_bias[layer].astype(F32)[:, None], (FOX_HEADS, LANES))
        c = _fox_decay(fox_f, bias_lanes)

        y_a = _hgrn2(hg_q, hg_f, hg_i, hg_g, hg_lb_logits.astype(F32),
                     hg_onorm_g[layer].astype(F32)[None, :], layer, batch, seq, ts)

        def rs(t):
            return t.reshape(batch, seq, FOX_W)

        y_b = _fox_attn(rs(fox_q), rs(fox_k), rs(fox_v), c,
                        jnp.tile(fox_q_norm_g[layer].astype(F32), 2)[None, :],
                        jnp.tile(fox_k_norm_g[layer].astype(F32), 2)[None, :], tq)

        x2d = _tail(
            x2d, y_a, y_b.reshape(batch * seq, FOX_W), gate_a, gate_b,
            p[layer].reshape(batch * seq, -1),
            w_branch_a[layer].astype(BF16), w_branch_b[layer].astype(BF16),
            w_out[layer].astype(BF16), norm_ffn_g[layer][None, :],
            w_ffn_gate[layer].astype(BF16), w_ffn_up[layer].astype(BF16),
            w_ffn_down[layer].astype(BF16), norm_ple_g[layer][None, :],
            w_ple_gate[layer].astype(BF16), w_ple_proj[layer].astype(BF16), tm)
    return x2d.reshape(batch, seq, d)
```

```python
import functools

import numpy as np
import jax
import jax.numpy as jnp
from jax import lax
from jax.experimental import pallas as pl
from jax.experimental.pallas import tpu as pltpu

F32 = jnp.float32
BF16 = jnp.bfloat16
EPS = 1e-6
LOG2E = 1.4426950408889634

HG_HEADS = 4
HG_DK = 128
HG_DV = 128
HG_W = HG_HEADS * HG_DK
HG_CHUNK = 64
FOX_HEADS = 8
FOX_DH = 64
FOX_W = FOX_HEADS * FOX_DH
LANES = 128
VMEM_LIMIT_BYTES = 56 * 1024 * 1024


def _dot(a, b):
    return jnp.dot(a, b, preferred_element_type=F32)


def _dot_nt(a, b):
    return lax.dot_general(a, b, (((1,), (1,)), ((), ())), preferred_element_type=F32)


def _dot_tn(a, b):
    return lax.dot_general(a, b, (((0,), (0,)), ((), ())), preferred_element_type=F32)


def _split3(a):
    hi = a.astype(BF16)
    r1 = a - hi.astype(F32)
    mid = r1.astype(BF16)
    lo = (r1 - mid.astype(F32)).astype(BF16)
    return hi, mid, lo


def _dot_exact_rhs(m01, a):
    hi, mid, lo = _split3(a)
    return _dot(m01, hi) + _dot(m01, mid) + _dot(m01, lo)


def _sigmoid(x):
    return 1.0 / (1.0 + jnp.exp(-x))


def _silu(x):
    return x * _sigmoid(x)


def _rms_scale(x):
    return lax.rsqrt(jnp.mean(x * x, axis=-1, keepdims=True) + EPS)


_IN_SEGS = (
    ("hg_q", HG_W, BF16),
    ("hg_f", HG_W, F32),
    ("hg_i", HG_W, BF16),
    ("hg_g", HG_W, BF16),
    ("fox_k", FOX_W, BF16),
    ("gate_a", 1024, BF16),
    ("gate_b", 1024, BF16),
    ("fox_f", LANES, F32),
)


def _in_proj_kernel(x_ref, g_ref, w_ref, wt_ref, *out_refs):
    x = x_ref[...]
    h = (x * _rms_scale(x) * g_ref[...]).astype(BF16)
    off = 0
    for (_, width, dtype), o_ref in zip(_IN_SEGS, out_refs[:-2]):
        o_ref[...] = _dot(h, w_ref[:, off:off + width]).astype(dtype)
        off += width
    out_refs[-2][0] = _dot_nt(wt_ref[0:FOX_W, :], h).astype(BF16)
    out_refs[-1][0] = _dot_nt(wt_ref[FOX_W:2 * FOX_W, :], h).astype(BF16)


def _in_proj(x2d, g, w_main, w_t, batch, seq, tm):
    tokens, d = x2d.shape
    per_b = seq // tm
    out_shape = [jax.ShapeDtypeStruct((tokens, wd), dt) for _, wd, dt in _IN_SEGS]
    out_specs = [pl.BlockSpec((tm, wd), lambda i: (i, 0)) for _, wd, _ in _IN_SEGS]
    for _ in range(2):
        out_shape.append(jax.ShapeDtypeStruct((batch, FOX_W, seq), BF16))
        out_specs.append(pl.BlockSpec((1, FOX_W, tm), lambda i: (i // per_b, 0, i % per_b)))
    return pl.pallas_call(
        _in_proj_kernel,
        grid=(tokens // tm,),
        in_specs=[
            pl.BlockSpec((tm, d), lambda i: (i, 0)),
            pl.BlockSpec((1, d), lambda i: (0, 0)),
            pl.BlockSpec(w_main.shape, lambda i: (0, 0), pipeline_mode=pl.Buffered(1)),
            pl.BlockSpec(w_t.shape, lambda i: (0, 0), pipeline_mode=pl.Buffered(1)),
        ],
        out_specs=out_specs,
        out_shape=out_shape,
        compiler_params=pltpu.CompilerParams(
            dimension_semantics=("parallel",), vmem_limit_bytes=VMEM_LIMIT_BYTES),
        name="in_proj",
    )(x2d, g, w_main, w_t)


def _fox_decay_kernel(f_ref, bias_ref, cs_ref):
    seq = f_ref.shape[1]
    r = lax.broadcasted_iota(jnp.int32, (LANES, LANES), 0)
    c = lax.broadcasted_iota(jnp.int32, (LANES, LANES), 1)
    lower = (c <= r).astype(BF16)
    place = [((c == r + FOX_HEADS * j) & (r < FOX_HEADS)).astype(BF16) for j in range(3)]
    bias = bias_ref[...]
    carry = jnp.zeros((1, LANES), F32)
    for j in range(seq // LANES):
        rows = slice(j * LANES, (j + 1) * LANES)
        z = f_ref[0, rows, :] + bias
        ls = jnp.minimum(z, 0.0) - jnp.log1p(jnp.exp(-jnp.abs(z)))
        cj = _dot_exact_rhs(lower, ls) + carry
        carry = cj[LANES - 1:LANES, :]
        hi, mid, lo = _split3(cj * (-LOG2E))
        cs_ref[0, rows, :] = (
            _dot(hi, place[0]) + _dot(mid, place[1]) + _dot(lo, place[2])).astype(BF16)


def _fox_decay(f_logits, bias_row):
    batch, seq, _ = f_logits.shape
    return pl.pallas_call(
        _fox_decay_kernel,
        grid=(batch,),
        in_specs=[
            pl.BlockSpec((1, seq, LANES), lambda b: (b, 0, 0)),
            pl.BlockSpec((1, LANES), lambda b: (0, 0)),
        ],
        out_specs=pl.BlockSpec((1, seq, LANES), lambda b: (b, 0, 0)),
        out_shape=jax.ShapeDtypeStruct((batch, seq, LANES), BF16),
        compiler_params=pltpu.CompilerParams(dimension_semantics=("parallel",)),
        name="fox_decay",
    )(f_logits, bias_row)


_HG_LEVELS = (32, 16, 8, 4, 2, 1)


def _hgrn2_decay_matrix():
    c = HG_CHUNK
    t = np.arange(c)[:, None]
    j = np.arange(c)[None, :]
    groups = [(j <= t), (j > t)]
    for n in _HG_LEVELS:
        ref = (t // (2 * n)) * (2 * n) + n - 1
        second = (t & n) != 0
        groups.append(np.where(second, (j > ref) & (j <= t), (j > t) & (j <= ref)))
    return np.concatenate(groups, axis=0).astype(np.float32)


def _hgrn2_kernel(q_ref, f_ref, i_ref, g_ref, lb_ref, og_ref, em_ref, y_ref, st_ref, *, layer):
    ts = q_ref.shape[0]
    c = HG_CHUNK

    @pl.when(pl.program_id(1) == 0)
    def _():
        st_ref[...] = jnp.zeros_like(st_ref)

    lbl = lb_ref[...]
    lbe = jnp.exp(lbl - jnp.max(lbl, axis=0, keepdims=True))
    lb = jnp.sum(lbe[0:layer + 1, :], axis=0, keepdims=True) / jnp.sum(lbe, axis=0, keepdims=True)

    em = em_ref[...]
    og = og_ref[...]
    row = lax.broadcasted_iota(jnp.int32, (c, HG_W), 0)
    tt = lax.broadcasted_iota(jnp.int32, (c, c), 0)
    ss = lax.broadcasted_iota(jnp.int32, (c, c), 1)
    txs = tt ^ ss
    causal = tt >= ss

    for ck in range(ts // c):
        rows = slice(ck * c, (ck + 1) * c)
        qf = _silu(q_ref[rows, :].astype(F32))
        fg = lb + (1.0 - lb) * _sigmoid(f_ref[rows, :])
        kf = 1.0 - fg
        logf = jnp.log(fg)
        decay = jnp.exp(_dot_exact_rhs(em, logf))
        v = i_ref[rows, :]
        gate = _silu(g_ref[rows, :].astype(F32))
        qe = (qf * decay[0:c]).astype(BF16)
        kd = (kf * decay[c:2 * c]).astype(BF16)
        e_last = decay[c - 1:c, :]
        mixed = [(qf.astype(BF16), kf.astype(BF16))]
        for li, n in enumerate(_HG_LEVELS):
            u = (jnp.where((row & n) != 0, qf, kf) * decay[(2 + li) * c:(3 + li) * c]).astype(BF16)
            mixed.append((u, u))
        for h in range(HG_HEADS):
            cols = slice(h * HG_DK, (h + 1) * HG_DK)
            qd, kdiag = mixed[0]
            scores = _dot_nt(qd[:, cols], kdiag[:, cols])
            for li, n in enumerate(reversed(_HG_LEVELS)):
                u = mixed[len(_HG_LEVELS) - li][0][:, cols]
                scores = jnp.where(txs >= n, _dot_nt(u, u), scores)
            scores = jnp.where(causal, scores, 0.0)
            st = st_ref[h]
            o = _dot_nt(qe[:, cols], st.astype(BF16)) + _dot(scores.astype(BF16), v[:, cols])
            st_ref[h] = st * e_last[:, cols] + _dot_tn(v[:, cols], kd[:, cols])
            o = o * _rms_scale(o) * og
            y_ref[rows, cols] = (o * gate[:, cols]).astype(y_ref.dtype)


def _hgrn2(q, f, i, g, lb_logits, o_gain, layer, batch, seq, ts):
    tokens = q.shape[0]
    per_b = seq // ts
    em = jnp.asarray(_hgrn2_decay_matrix(), dtype=BF16)
    tile = pl.BlockSpec((ts, HG_W), lambda b, s: (b * per_b + s, 0))
    return pl.pallas_call(
        functools.partial(_hgrn2_kernel, layer=layer),
        grid=(batch, per_b),
        in_specs=[
            tile, tile, tile, tile,
            pl.BlockSpec(lb_logits.shape, lambda b, s: (0, 0)),
            pl.BlockSpec((1, HG_DV), lambda b, s: (0, 0)),
            pl.BlockSpec(em.shape, lambda b, s: (0, 0)),
        ],
        out_specs=tile,
        out_shape=jax.ShapeDtypeStruct((tokens, HG_W), BF16),
        scratch_shapes=[pltpu.VMEM((HG_HEADS, HG_DV, HG_DK), F32)],
        compiler_params=pltpu.CompilerParams(dimension_semantics=("parallel", "arbitrary")),
        name="hgrn2",
    )(q, f, i, g, lb_logits, o_gain, em)


_BIAS_TERMS = 3


def _pair_norm(x, gain):
    r = lax.broadcasted_iota(jnp.int32, (LANES, LANES), 0)
    c = lax.broadcasted_iota(jnp.int32, (LANES, LANES), 1)
    same_head = (r // FOX_DH == c // FOX_DH).astype(BF16)
    sq = x * x
    hi = sq.astype(BF16)
    lo = (sq - hi.astype(F32)).astype(BF16)
    ssum = _dot(hi, same_head) + _dot(lo, same_head)
    return x * lax.rsqrt(ssum * (1.0 / FOX_DH) + EPS) * gain


def _fox_attn_kernel(qt_ref, k_ref, vt_ref, cs_ref, qg_ref, kg_ref, o_ref,
                     kaug_ref, vaug_ref, acc_ref, s_ref, m_ref, *, tq, tk):
    seq = k_ref.shape[1]
    pair = pl.program_id(1)
    i = pl.program_id(2)

    @pl.when(i == 0)
    def _():
        kg = kg_ref[...]
        r = lax.broadcasted_iota(jnp.int32, (LANES, LANES), 0)
        c = lax.broadcasted_iota(jnp.int32, (LANES, LANES), 1)
        src = FOX_HEADS * (c % _BIAS_TERMS) + 2 * pair + c // _BIAS_TERMS
        pick = ((r == src) & (c < 2 * _BIAS_TERMS)).astype(BF16)
        ones = jnp.ones((FOX_DH, 512), BF16)
        for r0 in range(0, seq, 512):
            rows = slice(r0, r0 + 512)
            kaug_ref[rows, 0:LANES] = _pair_norm(k_ref[0, rows, :].astype(F32), kg).astype(BF16)
            kaug_ref[rows, LANES:2 * LANES] = _dot(cs_ref[0, rows, :], pick).astype(BF16)
            vaug_ref[0, 0:FOX_DH, rows] = vt_ref[0, 0:FOX_DH, rows]
            vaug_ref[0, FOX_DH:LANES, rows] = ones
            vaug_ref[1, 0:FOX_DH, rows] = ones
            vaug_ref[1, FOX_DH:LANES, rows] = vt_ref[0, FOX_DH:LANES, rows]

    qt = qt_ref[0].astype(F32)

    def head_norm(xh):
        return xh * lax.rsqrt(jnp.mean(xh * xh, axis=0, keepdims=True) + EPS)

    qn = jnp.concatenate([head_norm(qt[0:FOX_DH]), head_norm(qt[FOX_DH:LANES])], axis=0)
    qn = qn * qg_ref[...] * (FOX_DH ** -0.5 * LOG2E)
    zeros = jnp.zeros((FOX_DH, tq), F32)
    ri = lax.broadcasted_iota(jnp.int32, (LANES, tq), 0)
    qa = []
    for hh in range(2):
        feats = [qn[0:FOX_DH], zeros] if hh == 0 else [zeros, qn[FOX_DH:LANES]]
        ind = ((ri >= _BIAS_TERMS * hh) & (ri < _BIAS_TERMS * (hh + 1))).astype(F32)
        qa.append(jnp.concatenate(feats + [ind], axis=0).astype(BF16))

    acc_ref[...] = jnp.zeros_like(acc_ref)
    key_i = lax.broadcasted_iota(jnp.int32, (tk, tq), 0)
    qry_i = lax.broadcasted_iota(jnp.int32, (tk, tq), 1)

    m_ref[...] = jnp.full(m_ref.shape, -jnp.inf, F32)

    def put_scores(kb, slot):
        k0 = pl.multiple_of(kb * tk, tk)
        ka = kaug_ref[pl.ds(k0, tk), :]
        for hh in range(2):
            s_ref[2 * slot + hh] = _dot(ka, qa[hh])

    def consume(kb, slot, masked):
        k0 = pl.multiple_of(kb * tk, tk)
        for hh in range(2):
            s = s_ref[2 * slot + hh]
            if masked:
                s = jnp.where(key_i <= qry_i, s, -jnp.inf)
            m_old = m_ref[hh]
            m_new = jnp.maximum(m_old, jnp.max(s, axis=0, keepdims=True))
            m_ref[hh] = m_new
            alpha = jnp.exp2(m_old - m_new)
            pt = jnp.exp2(s - m_new).astype(BF16)
            acc_ref[hh] = alpha * acc_ref[hh] + _dot(vaug_ref[hh, :, pl.ds(k0, tk)], pt)

    def two_blocks(j, carry):
        kb = 2 * j
        put_scores(kb + 1, 1)
        consume(kb, 0, False)
        put_scores(kb + 2, 0)
        consume(kb + 1, 1, False)
        return carry

    put_scores(0, 0)
    lax.fori_loop(0, i // 2, two_blocks, 0)

    @pl.when(i % 2 == 1)
    def _():
        put_scores(i, 1)
        consume(i - 1, 0, False)
        consume(i, 1, True)

    @pl.when(i % 2 == 0)
    def _():
        consume(i, 0, True)

    a0 = acc_ref[0]
    a1 = acc_ref[1]
    out_t = jnp.concatenate(
        [a0[0:FOX_DH] / a0[FOX_DH:FOX_DH + 1], a1[FOX_DH:LANES] / a1[0:1]], axis=0)
    o_ref[0] = out_t.T.astype(o_ref.dtype)


def _fox_attn(qt, k, vt, cs, q_gain_cols, k_gain2, tq):
    batch, seq, _ = k.shape
    pairs = FOX_HEADS // 2
    kernel = functools.partial(_fox_attn_kernel, tq=tq, tk=tq)
    return pl.pallas_call(
        kernel,
        grid=(batch, pairs, seq // tq),
        in_specs=[
            pl.BlockSpec((1, LANES, tq), lambda b, p, i: (b, p, i)),
            pl.BlockSpec((1, seq, LANES), lambda b, p, i: (b, 0, p)),
            pl.BlockSpec((1, LANES, seq), lambda b, p, i: (b, p, 0)),
            pl.BlockSpec((1, seq, LANES), lambda b, p, i: (b, 0, 0)),
            pl.BlockSpec((LANES, tq), lambda b, p, i: (0, 0)),
            pl.BlockSpec((1, LANES), lambda b, p, i: (0, 0)),
        ],
        out_specs=pl.BlockSpec((1, tq, LANES), lambda b, p, i: (b, i, p)),
        out_shape=jax.ShapeDtypeStruct((batch, seq, FOX_W), BF16),
        scratch_shapes=[
            pltpu.VMEM((seq, 2 * LANES), BF16),
            pltpu.VMEM((2, LANES, seq), BF16),
            pltpu.VMEM((2, LANES, tq), F32),
            pltpu.VMEM((4, tq, tq), F32),
            pltpu.VMEM((2, 1, tq), F32),
        ],
        compiler_params=pltpu.CompilerParams(
            dimension_semantics=("parallel", "parallel", "arbitrary")),
        name="fox_attn",
    )(qt, k, vt, cs, q_gain_cols, k_gain2)


def _ff_chunks(d_ff):
    chunks, off = [], 0
    while off < d_ff:
        wd = min(1024, d_ff - off)
        chunks.append((off, wd))
        off += wd
    return chunks


def _tail_kernel(x_ref, ya_ref, yb_ref, ga_ref, gb_ref, p_ref, wa_ref, wb_ref, wo_ref,
                 gf_ref, wg_ref, wu_ref, wd_ref, gp_ref, wpg_ref, wpp_ref, o_ref):
    ga = _sigmoid(ga_ref[...].astype(F32))
    gb = _sigmoid(gb_ref[...].astype(F32))
    merged = ga * _dot(ya_ref[...], wa_ref[...]) + gb * _dot(yb_ref[...], wb_ref[...])
    x = x_ref[...] + _dot(merged.astype(BF16), wo_ref[...])

    hf = (x * _rms_scale(x) * gf_ref[...]).astype(BF16)
    ffn = None
    for off, wd in _ff_chunks(wg_ref.shape[1]):
        act = _silu(_dot(hf, wg_ref[:, off:off + wd])) * _dot(hf, wu_ref[:, off:off + wd])
        part = _dot(act.astype(BF16), wd_ref[off:off + wd, :])
        ffn = part if ffn is None else ffn + part
    x = x + ffn

    hp = (x * _rms_scale(x) * gp_ref[...]).astype(BF16)
    emb = _dot(p_ref[...].astype(BF16), wpp_ref[...])
    o_ref[...] = x + _sigmoid(_dot(hp, wpg_ref[...])) * emb


def _tail(x2d, ya, yb, ga, gb, p2d, wa, wb, wo, gf, wg, wu, wd, gp, wpg, wpp, tm):
    tokens, d = x2d.shape

    def tile(width):
        return pl.BlockSpec((tm, width), lambda i: (i, 0))

    def whole(arr):
        return pl.BlockSpec(arr.shape, lambda i: (0, 0), pipeline_mode=pl.Buffered(1))

    return pl.pallas_call(
        _tail_kernel,
        grid=(tokens // tm,),
        in_specs=[
            tile(d), tile(ya.shape[1]), tile(yb.shape[1]), tile(d), tile(d), tile(p2d.shape[1]),
            whole(wa), whole(wb), whole(wo), whole(gf), whole(wg), whole(wu), whole(wd),
            whole(gp), whole(wpg), whole(wpp),
        ],
        out_specs=tile(d),
        out_shape=jax.ShapeDtypeStruct((tokens, d), F32),
        compiler_params=pltpu.CompilerParams(
            dimension_semantics=("parallel",), vmem_limit_bytes=VMEM_LIMIT_BYTES),
        name="tail",
    )(x2d, ya, yb, ga, gb, p2d, wa, wb, wo, gf, wg, wu, wd, gp, wpg, wpp)


def _tile(n, pref):
    return pref if n % pref == 0 else n


def kernel(x, p, norm_mix_g, w_in, hg_lb_logits, hg_onorm_g, fox_f_bias, fox_q_norm_g,
           fox_k_norm_g, w_branch_a, w_branch_b, w_out, norm_ffn_g, w_ffn_gate, w_ffn_up,
           w_ffn_down, norm_ple_g, w_ple_gate, w_ple_proj):
    batch, seq, d = x.shape
    depth = p.shape[0]
    off_fox_q = 4 * HG_W
    off_fox_k = off_fox_q + FOX_W
    off_fox_v = off_fox_k + FOX_W
    off_fox_f = off_fox_v + FOX_W
    off_gate = off_fox_f + FOX_HEADS
    assert w_in.shape[2] == off_gate + 2 * d and d == 1024
    assert seq % 512 == 0

    tm = 512
    ts = 128
    tq = 256

    x2d = x.reshape(batch * seq, d)
    for layer in range(depth):
        wl = w_in[layer]
        w_main = jnp.concatenate(
            [wl[:, :off_fox_q], wl[:, off_fox_k:off_fox_v], wl[:, off_gate:],
             jnp.pad(wl[:, off_fox_f:off_gate], ((0, 0), (0, LANES - FOX_HEADS)))],
            axis=1).astype(BF16)
        w_t = jnp.concatenate(
            [wl[:, off_fox_q:off_fox_k], wl[:, off_fox_v:off_fox_f]], axis=1).T.astype(BF16)
        (hg_q, hg_f, hg_i, hg_g, fox_k, gate_a, gate_b, fox_f, fox_qt, fox_vt) = _in_proj(
            x2d, norm_mix_g[layer][None, :], w_main, w_t, batch, seq, tm)

        bias_row = jnp.pad(fox_f_bias[layer].astype(F32), (0, LANES - FOX_HEADS))[None, :]
        cs = _fox_decay(fox_f.reshape(batch, seq, LANES), bias_row)

        y_a = _hgrn2(hg_q, hg_f, hg_i, hg_g, hg_lb_logits.astype(F32),
                     hg_onorm_g[layer].astype(F32)[None, :], layer, batch, seq, ts)

        q_gain_cols = jnp.broadcast_to(
            jnp.tile(fox_q_norm_g[layer].astype(F32), 2)[:, None], (LANES, tq))
        y_b = _fox_attn(fox_qt, fox_k.reshape(batch, seq, FOX_W), fox_vt, cs, q_gain_cols,
                        jnp.tile(fox_k_norm_g[layer].astype(F32), 2)[None, :], tq)

        x2d = _tail(
            x2d, y_a, y_b.reshape(batch * seq, FOX_W), gate_a, gate_b,
            p[layer].reshape(batch * seq, -1),
            w_branch_a[layer].astype(BF16), w_branch_b[layer].astype(BF16),
            w_out[layer].astype(BF16), norm_ffn_g[layer][None, :],
            w_ffn_gate[layer].astype(BF16), w_ffn_up[layer].astype(BF16),
            w_ffn_down[layer].astype(BF16), norm_ple_g[layer][None, :],
            w_ple_gate[layer].astype(BF16), w_ple_proj[layer].astype(BF16), tm)
    return x2d.reshape(batch, seq, d)
```

```python
import functools

import numpy as np
import jax
import jax.numpy as jnp
from jax import lax
from jax.experimental import pallas as pl
from jax.experimental.pallas import tpu as pltpu

F32 = jnp.float32
BF16 = jnp.bfloat16
EPS = 1e-6
LOG2E = 1.4426950408889634

HG_HEADS = 4
HG_DK = 128
HG_DV = 128
HG_W = HG_HEADS * HG_DK
HG_CHUNK = 64
FOX_HEADS = 8
FOX_DH = 64
FOX_W = FOX_HEADS * FOX_DH
LANES = 128
VMEM_LIMIT_BYTES = 56 * 1024 * 1024


def _dot(a, b):
    return jnp.dot(a, b, preferred_element_type=F32)


def _dot_nt(a, b):
    return lax.dot_general(a, b, (((1,), (1,)), ((), ())), preferred_element_type=F32)


def _dot_tn(a, b):
    return lax.dot_general(a, b, (((0,), (0,)), ((), ())), preferred_element_type=F32)


def _split3(a):
    hi = a.astype(BF16)
    r1 = a - hi.astype(F32)
    mid = r1.astype(BF16)
    lo = (r1 - mid.astype(F32)).astype(BF16)
    return hi, mid, lo


def _dot_exact_rhs(m01, a):
    hi, mid, lo = _split3(a)
    return _dot(m01, hi) + _dot(m01, mid) + _dot(m01, lo)


def _sigmoid(x):
    return 1.0 / (1.0 + jnp.exp2(x * (-LOG2E)))


def _silu(x):
    return x * _sigmoid(x)


def _rms_scale(x):
    return lax.rsqrt(jnp.mean(x * x, axis=-1, keepdims=True) + EPS)


_IN_SEGS = (
    ("hg_q", HG_W, BF16),
    ("hg_f", HG_W, F32),
    ("hg_i", HG_W, BF16),
    ("hg_g", HG_W, BF16),
    ("fox_k", FOX_W, BF16),
    ("gate_a", 1024, BF16),
    ("gate_b", 1024, BF16),
    ("fox_f", LANES, F32),
)


def _in_proj_kernel(x_ref, g_ref, w_ref, wt_ref, *out_refs):
    x = x_ref[...]
    h = (x * _rms_scale(x) * g_ref[...]).astype(BF16)
    off = 0
    for (_, width, dtype), o_ref in zip(_IN_SEGS, out_refs[:-2]):
        o_ref[...] = _dot(h, w_ref[:, off:off + width]).astype(dtype)
        off += width
    out_refs[-2][0] = _dot_nt(wt_ref[0:FOX_W, :], h).astype(BF16)
    out_refs[-1][0] = _dot_nt(wt_ref[FOX_W:2 * FOX_W, :], h).astype(BF16)


def _in_proj(x2d, g, w_main, w_t, batch, seq, tm):
    tokens, d = x2d.shape
    per_b = seq // tm
    out_shape = [jax.ShapeDtypeStruct((tokens, wd), dt) for _, wd, dt in _IN_SEGS]
    out_specs = [pl.BlockSpec((tm, wd), lambda i: (i, 0)) for _, wd, _ in _IN_SEGS]
    for _ in range(2):
        out_shape.append(jax.ShapeDtypeStruct((batch, FOX_W, seq), BF16))
        out_specs.append(pl.BlockSpec((1, FOX_W, tm), lambda i: (i // per_b, 0, i % per_b)))
    return pl.pallas_call(
        _in_proj_kernel,
        grid=(tokens // tm,),
        in_specs=[
            pl.BlockSpec((tm, d), lambda i: (i, 0)),
            pl.BlockSpec((1, d), lambda i: (0, 0)),
            pl.BlockSpec(w_main.shape, lambda i: (0, 0), pipeline_mode=pl.Buffered(1)),
            pl.BlockSpec(w_t.shape, lambda i: (0, 0), pipeline_mode=pl.Buffered(1)),
        ],
        out_specs=out_specs,
        out_shape=out_shape,
        compiler_params=pltpu.CompilerParams(
            dimension_semantics=("parallel",), vmem_limit_bytes=VMEM_LIMIT_BYTES),
        name="in_proj",
    )(x2d, g, w_main, w_t)


def _fox_decay_kernel(f_ref, bias_ref, cs_ref):
    seq = f_ref.shape[1]
    r = lax.broadcasted_iota(jnp.int32, (LANES, LANES), 0)
    c = lax.broadcasted_iota(jnp.int32, (LANES, LANES), 1)
    lower = (c <= r).astype(BF16)
    place = [((c == r + FOX_HEADS * j) & (r < FOX_HEADS)).astype(BF16) for j in range(3)]
    bias = bias_ref[...]
    carry = jnp.zeros((1, LANES), F32)
    for j in range(seq // LANES):
        rows = slice(j * LANES, (j + 1) * LANES)
        z = f_ref[0, rows, :] + bias
        ls = jnp.minimum(z, 0.0) - jnp.log1p(jnp.exp(-jnp.abs(z)))
        cj = _dot_exact_rhs(lower, ls) + carry
        carry = cj[LANES - 1:LANES, :]
        hi, mid, lo = _split3(cj * (-LOG2E))
        cs_ref[0, rows, :] = (
            _dot(hi, place[0]) + _dot(mid, place[1]) + _dot(lo, place[2])).astype(BF16)


def _fox_decay(f_logits, bias_row):
    batch, seq, _ = f_logits.shape
    return pl.pallas_call(
        _fox_decay_kernel,
        grid=(batch,),
        in_specs=[
            pl.BlockSpec((1, seq, LANES), lambda b: (b, 0, 0)),
            pl.BlockSpec((1, LANES), lambda b: (0, 0)),
        ],
        out_specs=pl.BlockSpec((1, seq, LANES), lambda b: (b, 0, 0)),
        out_shape=jax.ShapeDtypeStruct((batch, seq, LANES), BF16),
        compiler_params=pltpu.CompilerParams(dimension_semantics=("parallel",)),
        name="fox_decay",
    )(f_logits, bias_row)


_HG_LEVELS = (32, 16, 8, 4, 2, 1)


def _hgrn2_decay_matrix():
    c = HG_CHUNK
    t = np.arange(c)[:, None]
    j = np.arange(c)[None, :]
    groups = [(j <= t), (j > t)]
    for n in _HG_LEVELS:
        ref = (t // (2 * n)) * (2 * n) + n - 1
        second = (t & n) != 0
        groups.append(np.where(second, (j > ref) & (j <= t), (j > t) & (j <= ref)))
    return np.concatenate(groups, axis=0).astype(np.float32)


def _hgrn2_kernel(q_ref, f_ref, i_ref, g_ref, lb_ref, og_ref, em_ref, y_ref, st_ref, *, layer):
    ts = q_ref.shape[0]
    c = HG_CHUNK

    @pl.when(pl.program_id(1) == 0)
    def _():
        st_ref[...] = jnp.zeros_like(st_ref)

    lbl = lb_ref[...]
    lbe = jnp.exp(lbl - jnp.max(lbl, axis=0, keepdims=True))
    lb = jnp.sum(lbe[0:layer + 1, :], axis=0, keepdims=True) / jnp.sum(lbe, axis=0, keepdims=True)

    em3 = em_ref[...]
    og = og_ref[...]
    row = lax.broadcasted_iota(jnp.int32, (c, HG_W), 0)
    sublanes = 8

    def pick_side(n, qf, kf):
        if n >= sublanes:
            return jnp.concatenate(
                [(qf if (r0 // n) % 2 else kf)[r0:r0 + n] for r0 in range(0, c, n)], axis=0)
        return jnp.where((row & n) != 0, qf, kf)

    tt = lax.broadcasted_iota(jnp.int32, (c, c), 0)
    ss = lax.broadcasted_iota(jnp.int32, (c, c), 1)
    txs = tt ^ ss
    level_of = [txs >= n for n in reversed(_HG_LEVELS)]
    causal = tt >= ss
    nck = ts // c
    heads = [slice(h * HG_DK, (h + 1) * HG_DK) for h in range(HG_HEADS)]

    qe, kd, e_last, mixed = [], [], [], []
    for ck in range(nck):
        rows = slice(ck * c, (ck + 1) * c)
        qf = _silu(q_ref[rows, :].astype(F32))
        fg = lb + (1.0 - lb) * _sigmoid(f_ref[rows, :])
        kf = 1.0 - fg
        terms = jnp.concatenate(_split3(jnp.log2(fg)), axis=0)
        decay = jnp.exp2(_dot(em3, terms))
        qe.append((qf * decay[0:c]).astype(BF16))
        kd.append((kf * decay[c:2 * c]).astype(BF16))
        e_last.append(decay[c - 1:c, :])
        us = [(qf.astype(BF16), kf.astype(BF16))]
        for li, n in enumerate(_HG_LEVELS):
            u = pick_side(n, qf, kf) * decay[(2 + li) * c:(3 + li) * c]
            us.append((u.astype(BF16),) * 2)
        mixed.append(us)

    scores = []
    for ck in range(nck):
        per_head = []
        for cols in heads:
            qd, kdiag = mixed[ck][0]
            sc = _dot_nt(qd[:, cols], kdiag[:, cols])
            for li in range(len(_HG_LEVELS)):
                u = mixed[ck][len(_HG_LEVELS) - li][0][:, cols]
                sc = jnp.where(level_of[li], _dot_nt(u, u), sc)
            per_head.append(jnp.where(causal, sc, 0.0).astype(BF16))
        scores.append(per_head)

    for ck in range(nck):
        rows = slice(ck * c, (ck + 1) * c)
        v = i_ref[rows, :]
        gate = _silu(g_ref[rows, :].astype(F32))
        for h, cols in enumerate(heads):
            st = st_ref[h]
            o = _dot_nt(qe[ck][:, cols], st.astype(BF16)) + _dot(scores[ck][h], v[:, cols])
            st_ref[h] = st * e_last[ck][:, cols] + _dot_tn(v[:, cols], kd[ck][:, cols])
            o = o * _rms_scale(o) * og
            y_ref[rows, cols] = (o * gate[:, cols]).astype(y_ref.dtype)


def _hgrn2(q, f, i, g, lb_logits, o_gain, layer, batch, seq, ts):
    tokens = q.shape[0]
    per_b = seq // ts
    em = jnp.asarray(np.tile(_hgrn2_decay_matrix(), (1, 3)), dtype=BF16)
    tile = pl.BlockSpec((ts, HG_W), lambda b, s: (b * per_b + s, 0))
    return pl.pallas_call(
        functools.partial(_hgrn2_kernel, layer=layer),
        grid=(batch, per_b),
        in_specs=[
            tile, tile, tile, tile,
            pl.BlockSpec(lb_logits.shape, lambda b, s: (0, 0)),
            pl.BlockSpec((1, HG_DV), lambda b, s: (0, 0)),
            pl.BlockSpec(em.shape, lambda b, s: (0, 0)),
        ],
        out_specs=tile,
        out_shape=jax.ShapeDtypeStruct((tokens, HG_W), BF16),
        scratch_shapes=[pltpu.VMEM((HG_HEADS, HG_DV, HG_DK), F32)],
        compiler_params=pltpu.CompilerParams(dimension_semantics=("parallel", "arbitrary")),
        name="hgrn2",
    )(q, f, i, g, lb_logits, o_gain, em)


_BIAS_TERMS = 3


def _pair_norm(x, gain):
    r = lax.broadcasted_iota(jnp.int32, (LANES, LANES), 0)
    c = lax.broadcasted_iota(jnp.int32, (LANES, LANES), 1)
    same_head = (r // FOX_DH == c // FOX_DH).astype(BF16)
    sq = x * x
    hi = sq.astype(BF16)
    lo = (sq - hi.astype(F32)).astype(BF16)
    ssum = _dot(hi, same_head) + _dot(lo, same_head)
    return x * lax.rsqrt(ssum * (1.0 / FOX_DH) + EPS) * gain


def _fox_attn_kernel(qt_ref, k_ref, vt_ref, cs_ref, qg_ref, kg_ref, o_ref,
                     kaug_ref, vaug_ref, acc_ref, s_ref, m_ref, *, tq, tk):
    seq = k_ref.shape[1]
    pair = pl.program_id(1)
    i = pl.program_id(2)

    @pl.when(i == 0)
    def _():
        kg = kg_ref[...]
        r = lax.broadcasted_iota(jnp.int32, (LANES, LANES), 0)
        c = lax.broadcasted_iota(jnp.int32, (LANES, LANES), 1)
        src = FOX_HEADS * (c % _BIAS_TERMS) + 2 * pair + c // _BIAS_TERMS
        pick = ((r == src) & (c < 2 * _BIAS_TERMS)).astype(BF16)
        ones = jnp.ones((FOX_DH, 512), BF16)
        for r0 in range(0, seq, 512):
            rows = slice(r0, r0 + 512)
            kaug_ref[rows, 0:LANES] = _pair_norm(k_ref[0, rows, :].astype(F32), kg).astype(BF16)
            kaug_ref[rows, LANES:2 * LANES] = _dot(cs_ref[0, rows, :], pick).astype(BF16)
            vaug_ref[0, 0:FOX_DH, rows] = vt_ref[0, 0:FOX_DH, rows]
            vaug_ref[0, FOX_DH:LANES, rows] = ones
            vaug_ref[1, 0:FOX_DH, rows] = ones
            vaug_ref[1, FOX_DH:LANES, rows] = vt_ref[0, FOX_DH:LANES, rows]

    qt = qt_ref[0].astype(F32)

    def head_norm(xh):
        return xh * lax.rsqrt(jnp.mean(xh * xh, axis=0, keepdims=True) + EPS)

    qn = jnp.concatenate([head_norm(qt[0:FOX_DH]), head_norm(qt[FOX_DH:LANES])], axis=0)
    qn = qn * qg_ref[...] * (FOX_DH ** -0.5 * LOG2E)
    zeros = jnp.zeros((FOX_DH, tq), F32)
    ri = lax.broadcasted_iota(jnp.int32, (LANES, tq), 0)
    qa = []
    for hh in range(2):
        feats = [qn[0:FOX_DH], zeros] if hh == 0 else [zeros, qn[FOX_DH:LANES]]
        ind = ((ri >= _BIAS_TERMS * hh) & (ri < _BIAS_TERMS * (hh + 1))).astype(F32)
        qa.append(jnp.concatenate(feats + [ind], axis=0).astype(BF16))

    acc_ref[...] = jnp.zeros_like(acc_ref)
    key_minus_qry = (lax.broadcasted_iota(jnp.int32, (tk, tq), 0)
                     - lax.broadcasted_iota(jnp.int32, (tk, tq), 1))

    m_ref[...] = jnp.full(m_ref.shape, -jnp.inf, F32)

    def put_scores(kb, slot):
        k0 = pl.multiple_of(kb * tk, tk)
        ka = kaug_ref[pl.ds(k0, tk), :]
        for hh in range(2):
            s_ref[2 * slot + hh] = _dot(ka, qa[hh])

    def consume(kb, slot, masked):
        k0 = pl.multiple_of(kb * tk, tk)
        for hh in range(2):
            s = s_ref[2 * slot + hh]
            if masked:
                s = jnp.where(key_minus_qry <= i * tq - k0, s, -jnp.inf)
            m_old = m_ref[hh]
            m_new = jnp.maximum(m_old, jnp.max(s, axis=0, keepdims=True))
            m_ref[hh] = m_new
            alpha = jnp.exp2(m_old - m_new)
            pt = jnp.exp2(s - m_new).astype(BF16)
            acc_ref[hh] = alpha * acc_ref[hh] + _dot(vaug_ref[hh, :, pl.ds(k0, tk)], pt)

    def two_blocks(j, carry):
        kb = 2 * j
        put_scores(kb + 1, 1)
        consume(kb, 0, False)
        put_scores(kb + 2, 0)
        consume(kb + 1, 1, False)
        return carry

    put_scores(0, 0)
    lax.fori_loop(0, i, two_blocks, 0)
    put_scores(2 * i + 1, 1)
    consume(2 * i, 0, True)
    consume(2 * i + 1, 1, True)

    a0 = acc_ref[0]
    a1 = acc_ref[1]
    out_t = jnp.concatenate(
        [a0[0:FOX_DH] / a0[FOX_DH:FOX_DH + 1], a1[FOX_DH:LANES] / a1[0:1]], axis=0)
    o_ref[0] = out_t.T.astype(o_ref.dtype)


def _fox_attn(qt, k, vt, cs, q_gain_cols, k_gain2, tq):
    batch, seq, _ = k.shape
    pairs = FOX_HEADS // 2
    tk = tq // 2
    kernel = functools.partial(_fox_attn_kernel, tq=tq, tk=tk)
    return pl.pallas_call(
        kernel,
        grid=(batch, pairs, seq // tq),
        in_specs=[
            pl.BlockSpec((1, LANES, tq), lambda b, p, i: (b, p, i)),
            pl.BlockSpec((1, seq, LANES), lambda b, p, i: (b, 0, p)),
            pl.BlockSpec((1, LANES, seq), lambda b, p, i: (b, p, 0)),
            pl.BlockSpec((1, seq, LANES), lambda b, p, i: (b, 0, 0)),
            pl.BlockSpec((LANES, tq), lambda b, p, i: (0, 0)),
            pl.BlockSpec((1, LANES), lambda b, p, i: (0, 0)),
        ],
        out_specs=pl.BlockSpec((1, tq, LANES), lambda b, p, i: (b, i, p)),
        out_shape=jax.ShapeDtypeStruct((batch, seq, FOX_W), BF16),
        scratch_shapes=[
            pltpu.VMEM((seq, 2 * LANES), BF16),
            pltpu.VMEM((2, LANES, seq), BF16),
            pltpu.VMEM((2, LANES, tq), F32),
            pltpu.VMEM((4, tk, tq), F32),
            pltpu.VMEM((2, 1, tq), F32),
        ],
        compiler_params=pltpu.CompilerParams(
            dimension_semantics=("parallel", "parallel", "arbitrary")),
        name="fox_attn",
    )(qt, k, vt, cs, q_gain_cols, k_gain2)


def _ff_chunks(d_ff):
    chunks, off = [], 0
    while off < d_ff:
        wd = min(1024, d_ff - off)
        chunks.append((off, wd))
        off += wd
    return chunks


def _tail_kernel(x_ref, ya_ref, yb_ref, ga_ref, gb_ref, p_ref, wa_ref, wb_ref, wo_ref,
                 gf_ref, wg_ref, wu_ref, wd_ref, gp_ref, wpg_ref, wpp_ref, o_ref):
    ga = _sigmoid(ga_ref[...].astype(F32))
    gb = _sigmoid(gb_ref[...].astype(F32))
    merged = ga * _dot(ya_ref[...], wa_ref[...]) + gb * _dot(yb_ref[...], wb_ref[...])
    x = x_ref[...] + _dot(merged.astype(BF16), wo_ref[...])

    hf = (x * _rms_scale(x) * gf_ref[...]).astype(BF16)
    ffn = None
    for off, wd in _ff_chunks(wg_ref.shape[1]):
        act = _silu(_dot(hf, wg_ref[:, off:off + wd])) * _dot(hf, wu_ref[:, off:off + wd])
        part = _dot(act.astype(BF16), wd_ref[off:off + wd, :])
        ffn = part if ffn is None else ffn + part
    x = x + ffn

    hp = (x * _rms_scale(x) * gp_ref[...]).astype(BF16)
    emb = _dot(p_ref[...].astype(BF16), wpp_ref[...])
    o_ref[...] = x + _sigmoid(_dot(hp, wpg_ref[...])) * emb


def _tail(x2d, ya, yb, ga, gb, p2d, wa, wb, wo, gf, wg, wu, wd, gp, wpg, wpp, tm):
    tokens, d = x2d.shape

    def tile(width):
        return pl.BlockSpec((tm, width), lambda i: (i, 0))

    def whole(arr):
        return pl.BlockSpec(arr.shape, lambda i: (0, 0), pipeline_mode=pl.Buffered(1))

    return pl.pallas_call(
        _tail_kernel,
        grid=(tokens // tm,),
        in_specs=[
            tile(d), tile(ya.shape[1]), tile(yb.shape[1]), tile(d), tile(d), tile(p2d.shape[1]),
            whole(wa), whole(wb), whole(wo), whole(gf), whole(wg), whole(wu), whole(wd),
            whole(gp), whole(wpg), whole(wpp),
        ],
        out_specs=tile(d),
        out_shape=jax.ShapeDtypeStruct((tokens, d), F32),
        compiler_params=pltpu.CompilerParams(
            dimension_semantics=("parallel",), vmem_limit_bytes=VMEM_LIMIT_BYTES),
        name="tail",
    )(x2d, ya, yb, ga, gb, p2d, wa, wb, wo, gf, wg, wu, wd, gp, wpg, wpp)


def _tile(n, pref):
    return pref if n % pref == 0 else n


def kernel(x, p, norm_mix_g, w_in, hg_lb_logits, hg_onorm_g, fox_f_bias, fox_q_norm_g,
           fox_k_norm_g, w_branch_a, w_branch_b, w_out, norm_ffn_g, w_ffn_gate, w_ffn_up,
           w_ffn_down, norm_ple_g, w_ple_gate, w_ple_proj):
    batch, seq, d = x.shape
    depth = p.shape[0]
    off_fox_q = 4 * HG_W
    off_fox_k = off_fox_q + FOX_W
    off_fox_v = off_fox_k + FOX_W
    off_fox_f = off_fox_v + FOX_W
    off_gate = off_fox_f + FOX_HEADS
    assert w_in.shape[2] == off_gate + 2 * d and d == 1024
    assert seq % 512 == 0

    tm = 512
    ts = 256
    tq = 512

    x2d = x.reshape(batch * seq, d)
    for layer in range(depth):
        wl = w_in[layer]
        w_main = jnp.concatenate(
            [wl[:, :off_fox_q], wl[:, off_fox_k:off_fox_v], wl[:, off_gate:],
             jnp.pad(wl[:, off_fox_f:off_gate], ((0, 0), (0, LANES - FOX_HEADS)))],
            axis=1).astype(BF16)
        w_t = jnp.concatenate(
            [wl[:, off_fox_q:off_fox_k], wl[:, off_fox_v:off_fox_f]], axis=1).T.astype(BF16)
        (hg_q, hg_f, hg_i, hg_g, fox_k, gate_a, gate_b, fox_f, fox_qt, fox_vt) = _in_proj(
            x2d, norm_mix_g[layer][None, :], w_main, w_t, batch, seq, tm)

        bias_row = jnp.pad(fox_f_bias[layer].astype(F32), (0, LANES - FOX_HEADS))[None, :]
        cs = _fox_decay(fox_f.reshape(batch, seq, LANES), bias_row)

        y_a = _hgrn2(hg_q, hg_f, hg_i, hg_g, hg_lb_logits.astype(F32),
                     hg_onorm_g[layer].astype(F32)[None, :], layer, batch, seq, ts)

        q_gain_cols = jnp.broadcast_to(
            jnp.tile(fox_q_norm_g[layer].astype(F32), 2)[:, None], (LANES, tq))
        y_b = _fox_attn(fox_qt, fox_k.reshape(batch, seq, FOX_W), fox_vt, cs, q_gain_cols,
                        jnp.tile(fox_k_norm_g[layer].astype(F32), 2)[None, :], tq)

        x2d = _tail(
            x2d, y_a, y_b.reshape(batch * seq, FOX_W), gate_a, gate_b,
            p[layer].reshape(batch * seq, -1),
            w_branch_a[layer].astype(BF16), w_branch_b[layer].astype(BF16),
            w_out[layer].astype(BF16), norm_ffn_g[layer][None, :],
            w_ffn_gate[layer].astype(BF16), w_ffn_up[layer].astype(BF16),
            w_ffn_down[layer].astype(BF16), norm_ple_g[layer][None, :],
            w_ple_gate[layer].astype(BF16), w_ple_proj[layer].astype(BF16), tm)
    return x2d.reshape(batch, seq, d)
```

```python
import functools

import numpy as np
import jax
import jax.numpy as jnp
from jax import lax
from jax.experimental import pallas as pl
from jax.experimental.pallas import tpu as pltpu

F32 = jnp.float32
BF16 = jnp.bfloat16
EPS = 1e-6
LOG2E = 1.4426950408889634

HG_HEADS = 4
HG_DK = 128
HG_DV = 128
HG_W = HG_HEADS * HG_DK
HG_CHUNK = 64
FOX_HEADS = 8
FOX_DH = 64
FOX_W = FOX_HEADS * FOX_DH
LANES = 128
VMEM_LIMIT_BYTES = 56 * 1024 * 1024


def _dot(a, b):
    return jnp.dot(a, b, preferred_element_type=F32)


def _dot_nt(a, b):
    return lax.dot_general(a, b, (((1,), (1,)), ((), ())), preferred_element_type=F32)


def _dot_tn(a, b):
    return lax.dot_general(a, b, (((0,), (0,)), ((), ())), preferred_element_type=F32)


def _split3(a):
    hi = a.astype(BF16)
    r1 = a - hi.astype(F32)
    mid = r1.astype(BF16)
    lo = (r1 - mid.astype(F32)).astype(BF16)
    return hi, mid, lo


def _dot_exact_rhs(m01, a):
    hi, mid, lo = _split3(a)
    return _dot(m01, hi) + _dot(m01, mid) + _dot(m01, lo)


def _sigmoid(x):
    return 1.0 / (1.0 + jnp.exp2(x * (-LOG2E)))


def _silu(x):
    return x * _sigmoid(x)


def _rms_scale(x):
    return lax.rsqrt(jnp.mean(x * x, axis=-1, keepdims=True) + EPS)


_IN_SEGS = (
    ("hg_q", HG_W, BF16),
    ("hg_f", HG_W, F32),
    ("hg_i", HG_W, BF16),
    ("hg_g", HG_W, BF16),
    ("fox_k", FOX_W, BF16),
    ("gate_a", 1024, BF16),
    ("gate_b", 1024, BF16),
    ("fox_f", LANES, F32),
)


def _in_proj_kernel(x_ref, g_ref, w_ref, wt_ref, *out_refs):
    x = x_ref[...]
    h = (x * _rms_scale(x) * g_ref[...]).astype(BF16)
    off = 0
    for (_, width, dtype), o_ref in zip(_IN_SEGS, out_refs[:-2]):
        o_ref[...] = _dot(h, w_ref[:, off:off + width]).astype(dtype)
        off += width
    out_refs[-2][0] = _dot_nt(wt_ref[0:FOX_W, :], h).astype(BF16)
    out_refs[-1][0] = _dot_nt(wt_ref[FOX_W:2 * FOX_W, :], h).astype(BF16)


def _in_proj(x2d, g, w_main, w_t, batch, seq, tm):
    tokens, d = x2d.shape
    per_b = seq // tm
    out_shape = [jax.ShapeDtypeStruct((tokens, wd), dt) for _, wd, dt in _IN_SEGS]
    out_specs = [pl.BlockSpec((tm, wd), lambda i: (i, 0)) for _, wd, _ in _IN_SEGS]
    for _ in range(2):
        out_shape.append(jax.ShapeDtypeStruct((batch, FOX_W, seq), BF16))
        out_specs.append(pl.BlockSpec((1, FOX_W, tm), lambda i: (i // per_b, 0, i % per_b)))
    return pl.pallas_call(
        _in_proj_kernel,
        grid=(tokens // tm,),
        in_specs=[
            pl.BlockSpec((tm, d), lambda i: (i, 0)),
            pl.BlockSpec((1, d), lambda i: (0, 0)),
            pl.BlockSpec(w_main.shape, lambda i: (0, 0), pipeline_mode=pl.Buffered(1)),
            pl.BlockSpec(w_t.shape, lambda i: (0, 0), pipeline_mode=pl.Buffered(1)),
        ],
        out_specs=out_specs,
        out_shape=out_shape,
        compiler_params=pltpu.CompilerParams(
            dimension_semantics=("parallel",), vmem_limit_bytes=VMEM_LIMIT_BYTES),
        name="in_proj",
    )(x2d, g, w_main, w_t)


def _fox_decay_kernel(f_ref, bias_ref, cs_ref):
    seq = f_ref.shape[1]
    r = lax.broadcasted_iota(jnp.int32, (LANES, LANES), 0)
    c = lax.broadcasted_iota(jnp.int32, (LANES, LANES), 1)
    lower = (c <= r).astype(BF16)
    place = [((c == r + FOX_HEADS * j) & (r < FOX_HEADS)).astype(BF16) for j in range(3)]
    bias = bias_ref[...]
    carry = jnp.zeros((1, LANES), F32)
    for j in range(seq // LANES):
        rows = slice(j * LANES, (j + 1) * LANES)
        z = f_ref[0, rows, :] + bias
        ls = jnp.minimum(z, 0.0) - jnp.log1p(jnp.exp(-jnp.abs(z)))
        cj = _dot_exact_rhs(lower, ls) + carry
        carry = cj[LANES - 1:LANES, :]
        hi, mid, lo = _split3(cj * (-LOG2E))
        cs_ref[0, rows, :] = (
            _dot(hi, place[0]) + _dot(mid, place[1]) + _dot(lo, place[2])).astype(BF16)


def _fox_decay(f_logits, bias_row):
    batch, seq, _ = f_logits.shape
    return pl.pallas_call(
        _fox_decay_kernel,
        grid=(batch,),
        in_specs=[
            pl.BlockSpec((1, seq, LANES), lambda b: (b, 0, 0)),
            pl.BlockSpec((1, LANES), lambda b: (0, 0)),
        ],
        out_specs=pl.BlockSpec((1, seq, LANES), lambda b: (b, 0, 0)),
        out_shape=jax.ShapeDtypeStruct((batch, seq, LANES), BF16),
        compiler_params=pltpu.CompilerParams(dimension_semantics=("parallel",)),
        name="fox_decay",
    )(f_logits, bias_row)


_HG_LEVELS = (32, 16, 8, 4, 2, 1)


def _hgrn2_decay_matrix():
    c = HG_CHUNK
    t = np.arange(c)[:, None]
    j = np.arange(c)[None, :]
    groups = [(j <= t), (j > t)]
    for n in _HG_LEVELS:
        ref = (t // (2 * n)) * (2 * n) + n - 1
        second = (t & n) != 0
        groups.append(np.where(second, (j > ref) & (j <= t), (j > t) & (j <= ref)))
    return np.concatenate(groups, axis=0).astype(np.float32)


def _hgrn2_kernel(q_ref, f_ref, i_ref, g_ref, lb_ref, og_ref, em_ref, y_ref, st_ref, *, layer):
    ts = q_ref.shape[0]
    c = HG_CHUNK

    @pl.when(pl.program_id(1) == 0)
    def _():
        st_ref[...] = jnp.zeros_like(st_ref)

    lbl = lb_ref[...]
    lbe = jnp.exp(lbl - jnp.max(lbl, axis=0, keepdims=True))
    lb = jnp.sum(lbe[0:layer + 1, :], axis=0, keepdims=True) / jnp.sum(lbe, axis=0, keepdims=True)

    em3 = em_ref[...]
    og = og_ref[...]
    row = lax.broadcasted_iota(jnp.int32, (c, HG_W), 0)
    sublanes = 8

    def pick_side(n, qf, kf):
        if n >= sublanes:
            return jnp.concatenate(
                [(qf if (r0 // n) % 2 else kf)[r0:r0 + n] for r0 in range(0, c, n)], axis=0)
        return jnp.where((row & n) != 0, qf, kf)

    tt = lax.broadcasted_iota(jnp.int32, (c, c), 0)
    ss = lax.broadcasted_iota(jnp.int32, (c, c), 1)
    txs = tt ^ ss
    level_of = [txs >= n for n in reversed(_HG_LEVELS)]
    causal = tt >= ss
    nck = ts // c
    heads = [slice(h * HG_DK, (h + 1) * HG_DK) for h in range(HG_HEADS)]

    qe, kd, e_last, mixed = [], [], [], []
    for ck in range(nck):
        rows = slice(ck * c, (ck + 1) * c)
        qf = _silu(q_ref[rows, :].astype(F32))
        fg = lb + (1.0 - lb) * _sigmoid(f_ref[rows, :])
        kf = 1.0 - fg
        terms = jnp.concatenate(_split3(jnp.log2(fg)), axis=0)
        decay = jnp.exp2(_dot(em3, terms))
        qe.append((qf * decay[0:c]).astype(BF16))
        kd.append((kf * decay[c:2 * c]).astype(BF16))
        e_last.append(decay[c - 1:c, :])
        us = [(qf.astype(BF16), kf.astype(BF16))]
        for li, n in enumerate(_HG_LEVELS):
            u = pick_side(n, qf, kf) * decay[(2 + li) * c:(3 + li) * c]
            us.append((u.astype(BF16),) * 2)
        mixed.append(us)

    scores = []
    for ck in range(nck):
        per_head = []
        for cols in heads:
            qd, kdiag = mixed[ck][0]
            sc = _dot_nt(qd[:, cols], kdiag[:, cols])
            for li in range(len(_HG_LEVELS)):
                u = mixed[ck][len(_HG_LEVELS) - li][0][:, cols]
                sc = jnp.where(level_of[li], _dot_nt(u, u), sc)
            per_head.append(jnp.where(causal, sc, 0.0).astype(BF16))
        scores.append(per_head)

    for ck in range(nck):
        rows = slice(ck * c, (ck + 1) * c)
        v = i_ref[rows, :]
        gate = _silu(g_ref[rows, :].astype(F32))
        for h, cols in enumerate(heads):
            st = st_ref[h]
            o = _dot_nt(qe[ck][:, cols], st.astype(BF16)) + _dot(scores[ck][h], v[:, cols])
            st_ref[h] = st * e_last[ck][:, cols] + _dot_tn(v[:, cols], kd[ck][:, cols])
            o = o * _rms_scale(o) * og
            y_ref[rows, cols] = (o * gate[:, cols]).astype(y_ref.dtype)


def _hgrn2(q, f, i, g, lb_logits, o_gain, layer, batch, seq, ts):
    tokens = q.shape[0]
    per_b = seq // ts
    em = jnp.asarray(np.tile(_hgrn2_decay_matrix(), (1, 3)), dtype=BF16)
    tile = pl.BlockSpec((ts, HG_W), lambda b, s: (b * per_b + s, 0))
    return pl.pallas_call(
        functools.partial(_hgrn2_kernel, layer=layer),
        grid=(batch, per_b),
        in_specs=[
            tile, tile, tile, tile,
            pl.BlockSpec(lb_logits.shape, lambda b, s: (0, 0)),
            pl.BlockSpec((1, HG_DV), lambda b, s: (0, 0)),
            pl.BlockSpec(em.shape, lambda b, s: (0, 0)),
        ],
        out_specs=tile,
        out_shape=jax.ShapeDtypeStruct((tokens, HG_W), BF16),
        scratch_shapes=[pltpu.VMEM((HG_HEADS, HG_DV, HG_DK), F32)],
        compiler_params=pltpu.CompilerParams(dimension_semantics=("parallel", "arbitrary")),
        name="hgrn2",
    )(q, f, i, g, lb_logits, o_gain, em)


_BIAS_TERMS = 3
_PAIRS_PER_STEP = 2


def _pair_norm(x, gain):
    r = lax.broadcasted_iota(jnp.int32, (LANES, LANES), 0)
    c = lax.broadcasted_iota(jnp.int32, (LANES, LANES), 1)
    same_head = (r // FOX_DH == c // FOX_DH).astype(BF16)
    sq = x * x
    hi = sq.astype(BF16)
    lo = (sq - hi.astype(F32)).astype(BF16)
    ssum = _dot(hi, same_head) + _dot(lo, same_head)
    return x * lax.rsqrt(ssum * (1.0 / FOX_DH) + EPS) * gain


def _fox_attn_kernel(qt_ref, k_ref, vt_ref, cs_ref, qg_ref, kg_ref, o_ref,
                     kaug_ref, vaug_ref, acc_ref, s_ref, m_ref, *, tq, tk):
    seq = k_ref.shape[1]
    i = pl.program_id(2)

    @pl.when(i == 0)
    def _():
        kg = kg_ref[...]
        r = lax.broadcasted_iota(jnp.int32, (LANES, LANES), 0)
        c = lax.broadcasted_iota(jnp.int32, (LANES, LANES), 1)
        ones = jnp.ones((FOX_DH, 512), BF16)
        for u in range(_PAIRS_PER_STEP):
            pair = _PAIRS_PER_STEP * pl.program_id(1) + u
            src = FOX_HEADS * (c % _BIAS_TERMS) + 2 * pair + c // _BIAS_TERMS
            pick = ((r == src) & (c < 2 * _BIAS_TERMS)).astype(BF16)
            f0 = u * LANES
            for r0 in range(0, seq, 512):
                rows = slice(r0, r0 + 512)
                kn = _pair_norm(k_ref[0, rows, f0:f0 + LANES].astype(F32), kg)
                kaug_ref[u, rows, 0:LANES] = kn.astype(BF16)
                kaug_ref[u, rows, LANES:2 * LANES] = _dot(cs_ref[0, rows, :], pick).astype(BF16)
                vaug_ref[2 * u, 0:FOX_DH, rows] = vt_ref[0, f0:f0 + FOX_DH, rows]
                vaug_ref[2 * u, FOX_DH:LANES, rows] = ones
                vaug_ref[2 * u + 1, 0:FOX_DH, rows] = ones
                vaug_ref[2 * u + 1, FOX_DH:LANES, rows] = vt_ref[0, f0 + FOX_DH:f0 + LANES, rows]

    def head_norm(xh):
        return xh * lax.rsqrt(jnp.mean(xh * xh, axis=0, keepdims=True) + EPS)

    zeros = jnp.zeros((FOX_DH, tq), F32)
    ri = lax.broadcasted_iota(jnp.int32, (LANES, tq), 0)
    qa = []
    for u in range(_PAIRS_PER_STEP):
        qt = qt_ref[0, u * LANES:(u + 1) * LANES, :].astype(F32)
        qn = jnp.concatenate([head_norm(qt[0:FOX_DH]), head_norm(qt[FOX_DH:LANES])], axis=0)
        qn = qn * qg_ref[...] * (FOX_DH ** -0.5 * LOG2E)
        for hh in range(2):
            feats = [qn[0:FOX_DH], zeros] if hh == 0 else [zeros, qn[FOX_DH:LANES]]
            ind = ((ri >= _BIAS_TERMS * hh) & (ri < _BIAS_TERMS * (hh + 1))).astype(F32)
            qa.append(jnp.concatenate(feats + [ind], axis=0).astype(BF16))
    chains = len(qa)

    acc_ref[...] = jnp.zeros_like(acc_ref)
    key_minus_qry = (lax.broadcasted_iota(jnp.int32, (tk, tq), 0)
                     - lax.broadcasted_iota(jnp.int32, (tk, tq), 1))

    m_ref[...] = jnp.full(m_ref.shape, -jnp.inf, F32)

    def put_scores(kb, slot):
        k0 = pl.multiple_of(kb * tk, tk)
        for u in range(_PAIRS_PER_STEP):
            ka = kaug_ref[u, pl.ds(k0, tk), :]
            for hh in range(2):
                s_ref[chains * slot + 2 * u + hh] = _dot(ka, qa[2 * u + hh])

    def consume(kb, slot, masked):
        k0 = pl.multiple_of(kb * tk, tk)
        for ch in range(chains):
            s = s_ref[chains * slot + ch]
            if masked:
                s = jnp.where(key_minus_qry <= i * tq - k0, s, -jnp.inf)
            m_old = m_ref[ch]
            m_new = jnp.maximum(m_old, jnp.max(s, axis=0, keepdims=True))
            m_ref[ch] = m_new
            alpha = jnp.exp2(m_old - m_new)
            pt = jnp.exp2(s - m_new).astype(BF16)
            acc_ref[ch] = alpha * acc_ref[ch] + _dot(vaug_ref[ch, :, pl.ds(k0, tk)], pt)

    def two_blocks(j, carry):
        kb = 2 * j
        put_scores(kb + 1, 1)
        consume(kb, 0, False)
        put_scores(kb + 2, 0)
        consume(kb + 1, 1, False)
        return carry

    put_scores(0, 0)
    lax.fori_loop(0, i, two_blocks, 0)
    put_scores(2 * i + 1, 1)
    consume(2 * i, 0, True)
    consume(2 * i + 1, 1, True)

    for u in range(_PAIRS_PER_STEP):
        a0 = acc_ref[2 * u]
        a1 = acc_ref[2 * u + 1]
        out_t = jnp.concatenate(
            [a0[0:FOX_DH] / a0[FOX_DH:FOX_DH + 1], a1[FOX_DH:LANES] / a1[0:1]], axis=0)
        o_ref[0, :, u * LANES:(u + 1) * LANES] = out_t.T.astype(o_ref.dtype)


def _fox_attn(qt, k, vt, cs, q_gain_cols, k_gain2, tq):
    batch, seq, _ = k.shape
    tk = tq // 2
    width = _PAIRS_PER_STEP * LANES
    chains = 2 * _PAIRS_PER_STEP
    kernel = functools.partial(_fox_attn_kernel, tq=tq, tk=tk)
    return pl.pallas_call(
        kernel,
        grid=(batch, FOX_W // width, seq // tq),
        in_specs=[
            pl.BlockSpec((1, width, tq), lambda b, p, i: (b, p, i)),
            pl.BlockSpec((1, seq, width), lambda b, p, i: (b, 0, p)),
            pl.BlockSpec((1, width, seq), lambda b, p, i: (b, p, 0)),
            pl.BlockSpec((1, seq, LANES), lambda b, p, i: (b, 0, 0)),
            pl.BlockSpec((LANES, tq), lambda b, p, i: (0, 0)),
            pl.BlockSpec((1, LANES), lambda b, p, i: (0, 0)),
        ],
        out_specs=pl.BlockSpec((1, tq, width), lambda b, p, i: (b, i, p)),
        out_shape=jax.ShapeDtypeStruct((batch, seq, FOX_W), BF16),
        scratch_shapes=[
            pltpu.VMEM((_PAIRS_PER_STEP, seq, 2 * LANES), BF16),
            pltpu.VMEM((chains, LANES, seq), BF16),
            pltpu.VMEM((chains, LANES, tq), F32),
            pltpu.VMEM((2 * chains, tk, tq), F32),
            pltpu.VMEM((chains, 1, tq), F32),
        ],
        compiler_params=pltpu.CompilerParams(
            dimension_semantics=("parallel", "parallel", "arbitrary"),
            vmem_limit_bytes=VMEM_LIMIT_BYTES),
        name="fox_attn",
    )(qt, k, vt, cs, q_gain_cols, k_gain2)


def _ff_chunks(d_ff):
    chunks, off = [], 0
    while off < d_ff:
        wd = min(1024, d_ff - off)
        chunks.append((off, wd))
        off += wd
    return chunks


def _tail_kernel(x_ref, ya_ref, yb_ref, ga_ref, gb_ref, p_ref, wa_ref, wb_ref, wo_ref,
                 gf_ref, wg_ref, wu_ref, wd_ref, gp_ref, wpg_ref, wpp_ref, o_ref):
    ga = _sigmoid(ga_ref[...].astype(F32))
    gb = _sigmoid(gb_ref[...].astype(F32))
    merged = ga * _dot(ya_ref[...], wa_ref[...]) + gb * _dot(yb_ref[...], wb_ref[...])
    x = x_ref[...] + _dot(merged.astype(BF16), wo_ref[...])

    hf = (x * _rms_scale(x) * gf_ref[...]).astype(BF16)
    ffn = None
    for off, wd in _ff_chunks(wg_ref.shape[1]):
        act = _silu(_dot(hf, wg_ref[:, off:off + wd])) * _dot(hf, wu_ref[:, off:off + wd])
        part = _dot(act.astype(BF16), wd_ref[off:off + wd, :])
        ffn = part if ffn is None else ffn + part
    x = x + ffn

    hp = (x * _rms_scale(x) * gp_ref[...]).astype(BF16)
    emb = _dot(p_ref[...].astype(BF16), wpp_ref[...])
    o_ref[...] = x + _sigmoid(_dot(hp, wpg_ref[...])) * emb


def _tail(x2d, ya, yb, ga, gb, p2d, wa, wb, wo, gf, wg, wu, wd, gp, wpg, wpp, tm):
    tokens, d = x2d.shape

    def tile(width):
        return pl.BlockSpec((tm, width), lambda i: (i, 0))

    def whole(arr):
        return pl.BlockSpec(arr.shape, lambda i: (0, 0), pipeline_mode=pl.Buffered(1))

    return pl.pallas_call(
        _tail_kernel,
        grid=(tokens // tm,),
        in_specs=[
            tile(d), tile(ya.shape[1]), tile(yb.shape[1]), tile(d), tile(d), tile(p2d.shape[1]),
            whole(wa), whole(wb), whole(wo), whole(gf), whole(wg), whole(wu), whole(wd),
            whole(gp), whole(wpg), whole(wpp),
        ],
        out_specs=tile(d),
        out_shape=jax.ShapeDtypeStruct((tokens, d), F32),
        compiler_params=pltpu.CompilerParams(
            dimension_semantics=("parallel",), vmem_limit_bytes=VMEM_LIMIT_BYTES),
        name="tail",
    )(x2d, ya, yb, ga, gb, p2d, wa, wb, wo, gf, wg, wu, wd, gp, wpg, wpp)


def _tile(n, pref):
    return pref if n % pref == 0 else n


def kernel(x, p, norm_mix_g, w_in, hg_lb_logits, hg_onorm_g, fox_f_bias, fox_q_norm_g,
           fox_k_norm_g, w_branch_a, w_branch_b, w_out, norm_ffn_g, w_ffn_gate, w_ffn_up,
           w_ffn_down, norm_ple_g, w_ple_gate, w_ple_proj):
    batch, seq, d = x.shape
    depth = p.shape[0]
    off_fox_q = 4 * HG_W
    off_fox_k = off_fox_q + FOX_W
    off_fox_v = off_fox_k + FOX_W
    off_fox_f = off_fox_v + FOX_W
    off_gate = off_fox_f + FOX_HEADS
    assert w_in.shape[2] == off_gate + 2 * d and d == 1024
    assert seq % 512 == 0

    tm = 512
    ts = 512
    tq = 512

    x2d = x.reshape(batch * seq, d)
    for layer in range(depth):
        wl = w_in[layer]
        w_main = jnp.concatenate(
            [wl[:, :off_fox_q], wl[:, off_fox_k:off_fox_v], wl[:, off_gate:],
             jnp.pad(wl[:, off_fox_f:off_gate], ((0, 0), (0, LANES - FOX_HEADS)))],
            axis=1).astype(BF16)
        w_t = jnp.concatenate(
            [wl[:, off_fox_q:off_fox_k], wl[:, off_fox_v:off_fox_f]], axis=1).T.astype(BF16)
        (hg_q, hg_f, hg_i, hg_g, fox_k, gate_a, gate_b, fox_f, fox_qt, fox_vt) = _in_proj(
            x2d, norm_mix_g[layer][None, :], w_main, w_t, batch, seq, tm)

        bias_row = jnp.pad(fox_f_bias[layer].astype(F32), (0, LANES - FOX_HEADS))[None, :]
        cs = _fox_decay(fox_f.reshape(batch, seq, LANES), bias_row)

        y_a = _hgrn2(hg_q, hg_f, hg_i, hg_g, hg_lb_logits.astype(F32),
                     hg_onorm_g[layer].astype(F32)[None, :], layer, batch, seq, ts)

        q_gain_cols = jnp.broadcast_to(
            jnp.tile(fox_q_norm_g[layer].astype(F32), 2)[:, None], (LANES, tq))
        y_b = _fox_attn(fox_qt, fox_k.reshape(batch, seq, FOX_W), fox_vt, cs, q_gain_cols,
                        jnp.tile(fox_k_norm_g[layer].astype(F32), 2)[None, :], tq)

        x2d = _tail(
            x2d, y_a, y_b.reshape(batch * seq, FOX_W), gate_a, gate_b,
            p[layer].reshape(batch * seq, -1),
            w_branch_a[layer].astype(BF16), w_branch_b[layer].astype(BF16),
            w_out[layer].astype(BF16), norm_ffn_g[layer][None, :],
            w_ffn_gate[layer].astype(BF16), w_ffn_up[layer].astype(BF16),
            w_ffn_down[layer].astype(BF16), norm_ple_g[layer][None, :],
            w_ple_gate[layer].astype(BF16), w_ple_proj[layer].astype(BF16), tm)
    return x2d.reshape(batch, seq, d)
```

```python
import functools

import numpy as np
import jax
import jax.numpy as jnp
from jax import lax
from jax.experimental import pallas as pl
from jax.experimental.pallas import tpu as pltpu

F32 = jnp.float32
BF16 = jnp.bfloat16
EPS = 1e-6
LOG2E = 1.4426950408889634

HG_HEADS = 4
HG_DK = 128
HG_DV = 128
HG_W = HG_HEADS * HG_DK
HG_CHUNK = 64
FOX_HEADS = 8
FOX_DH = 64
FOX_W = FOX_HEADS * FOX_DH
LANES = 128
VMEM_LIMIT_BYTES = 56 * 1024 * 1024


def _dot(a, b):
    return jnp.dot(a, b, preferred_element_type=F32)


def _dot_nt(a, b):
    return lax.dot_general(a, b, (((1,), (1,)), ((), ())), preferred_element_type=F32)


def _dot_tn(a, b):
    return lax.dot_general(a, b, (((0,), (0,)), ((), ())), preferred_element_type=F32)


def _split3(a):
    hi = a.astype(BF16)
    r1 = a - hi.astype(F32)
    mid = r1.astype(BF16)
    lo = (r1 - mid.astype(F32)).astype(BF16)
    return hi, mid, lo


def _dot_exact_rhs(m01, a):
    hi, mid, lo = _split3(a)
    return _dot(m01, hi) + _dot(m01, mid) + _dot(m01, lo)


def _sigmoid(x):
    return 1.0 / (1.0 + jnp.exp2(x * (-LOG2E)))


def _silu(x):
    return x * _sigmoid(x)


def _rms_scale(x):
    return lax.rsqrt(jnp.mean(x * x, axis=-1, keepdims=True) + EPS)


_IN_SEGS = (
    ("hg_q", HG_W, BF16),
    ("hg_f", HG_W, F32),
    ("hg_i", HG_W, BF16),
    ("hg_g", HG_W, BF16),
    ("fox_k", FOX_W, BF16),
    ("gate_a", 1024, BF16),
    ("gate_b", 1024, BF16),
    ("fox_f", LANES, F32),
)


def _in_proj_kernel(x_ref, g_ref, w_ref, wt_ref, *out_refs):
    x = x_ref[...]
    h = (x * _rms_scale(x) * g_ref[...]).astype(BF16)
    off = 0
    for (_, width, dtype), o_ref in zip(_IN_SEGS, out_refs[:-2]):
        o_ref[...] = _dot(h, w_ref[:, off:off + width]).astype(dtype)
        off += width
    out_refs[-2][0] = _dot_nt(wt_ref[0:FOX_W, :], h).astype(BF16)
    out_refs[-1][0] = _dot_nt(wt_ref[FOX_W:2 * FOX_W, :], h).astype(BF16)


def _in_proj(x2d, g, w_main, w_t, batch, seq, tm):
    tokens, d = x2d.shape
    per_b = seq // tm
    out_shape = [jax.ShapeDtypeStruct((tokens, wd), dt) for _, wd, dt in _IN_SEGS]
    out_specs = [pl.BlockSpec((tm, wd), lambda i: (i, 0)) for _, wd, _ in _IN_SEGS]
    for _ in range(2):
        out_shape.append(jax.ShapeDtypeStruct((batch, FOX_W, seq), BF16))
        out_specs.append(pl.BlockSpec((1, FOX_W, tm), lambda i: (i // per_b, 0, i % per_b)))
    return pl.pallas_call(
        _in_proj_kernel,
        grid=(tokens // tm,),
        in_specs=[
            pl.BlockSpec((tm, d), lambda i: (i, 0)),
            pl.BlockSpec((1, d), lambda i: (0, 0)),
            pl.BlockSpec(w_main.shape, lambda i: (0, 0), pipeline_mode=pl.Buffered(1)),
            pl.BlockSpec(w_t.shape, lambda i: (0, 0), pipeline_mode=pl.Buffered(1)),
        ],
        out_specs=out_specs,
        out_shape=out_shape,
        compiler_params=pltpu.CompilerParams(
            dimension_semantics=("parallel",), vmem_limit_bytes=VMEM_LIMIT_BYTES),
        name="in_proj",
    )(x2d, g, w_main, w_t)


def _fox_decay_kernel(f_ref, bias_ref, cs_ref):
    seq = f_ref.shape[1]
    r = lax.broadcasted_iota(jnp.int32, (LANES, LANES), 0)
    c = lax.broadcasted_iota(jnp.int32, (LANES, LANES), 1)
    lower = (c <= r).astype(BF16)
    place = [((c == r + FOX_HEADS * j) & (r < FOX_HEADS)).astype(BF16) for j in range(3)]
    bias = bias_ref[...]
    carry = jnp.zeros((1, LANES), F32)
    for j in range(seq // LANES):
        rows = slice(j * LANES, (j + 1) * LANES)
        z = f_ref[0, rows, :] + bias
        ls = jnp.minimum(z, 0.0) - jnp.log1p(jnp.exp(-jnp.abs(z)))
        cj = _dot_exact_rhs(lower, ls) + carry
        carry = cj[LANES - 1:LANES, :]
        hi, mid, lo = _split3(cj * (-LOG2E))
        cs_ref[0, rows, :] = (
            _dot(hi, place[0]) + _dot(mid, place[1]) + _dot(lo, place[2])).astype(BF16)


def _fox_decay(f_logits, bias_row):
    batch, seq, _ = f_logits.shape
    return pl.pallas_call(
        _fox_decay_kernel,
        grid=(batch,),
        in_specs=[
            pl.BlockSpec((1, seq, LANES), lambda b: (b, 0, 0)),
            pl.BlockSpec((1, LANES), lambda b: (0, 0)),
        ],
        out_specs=pl.BlockSpec((1, seq, LANES), lambda b: (b, 0, 0)),
        out_shape=jax.ShapeDtypeStruct((batch, seq, LANES), BF16),
        compiler_params=pltpu.CompilerParams(dimension_semantics=("parallel",)),
        name="fox_decay",
    )(f_logits, bias_row)


_HG_LEVELS = (32, 16, 8, 4, 2, 1)


def _hgrn2_decay_matrix():
    c = HG_CHUNK
    t = np.arange(c)[:, None]
    j = np.arange(c)[None, :]
    groups = [(j <= t), (j > t)]
    for n in _HG_LEVELS:
        ref = (t // (2 * n)) * (2 * n) + n - 1
        second = (t & n) != 0
        groups.append(np.where(second, (j > ref) & (j <= t), (j > t) & (j <= ref)))
    return np.concatenate(groups, axis=0).astype(np.float32)


def _hgrn2_kernel(q_ref, f_ref, i_ref, g_ref, lb_ref, og_ref, em_ref, y_ref, st_ref, *, layer):
    ts = q_ref.shape[0]
    c = HG_CHUNK

    @pl.when(pl.program_id(1) == 0)
    def _():
        st_ref[...] = jnp.zeros_like(st_ref)

    lbl = lb_ref[...]
    lbe = jnp.exp(lbl - jnp.max(lbl, axis=0, keepdims=True))
    lb = jnp.sum(lbe[0:layer + 1, :], axis=0, keepdims=True) / jnp.sum(lbe, axis=0, keepdims=True)

    em3 = em_ref[...]
    og = og_ref[...]
    row = lax.broadcasted_iota(jnp.int32, (c, HG_W), 0)
    sublanes = 8

    def pick_side(n, qf, kf):
        if n >= sublanes:
            return jnp.concatenate(
                [(qf if (r0 // n) % 2 else kf)[r0:r0 + n] for r0 in range(0, c, n)], axis=0)
        return jnp.where((row & n) != 0, qf, kf)

    tt = lax.broadcasted_iota(jnp.int32, (c, c), 0)
    ss = lax.broadcasted_iota(jnp.int32, (c, c), 1)
    txs = tt ^ ss
    level_of = [txs >= n for n in reversed(_HG_LEVELS)]
    causal = tt >= ss
    nck = ts // c
    heads = [slice(h * HG_DK, (h + 1) * HG_DK) for h in range(HG_HEADS)]

    qe, kd, e_last, mixed = [], [], [], []
    for ck in range(nck):
        rows = slice(ck * c, (ck + 1) * c)
        qf = _silu(q_ref[rows, :].astype(F32))
        fg = lb + (1.0 - lb) * _sigmoid(f_ref[rows, :])
        kf = 1.0 - fg
        terms = jnp.concatenate(_split3(jnp.log2(fg)), axis=0)
        decay = jnp.exp2(_dot(em3, terms))
        qe.append((qf * decay[0:c]).astype(BF16))
        kd.append((kf * decay[c:2 * c]).astype(BF16))
        e_last.append(decay[c - 1:c, :])
        us = [(qf.astype(BF16), kf.astype(BF16))]
        for li, n in enumerate(_HG_LEVELS):
            u = pick_side(n, qf, kf) * decay[(2 + li) * c:(3 + li) * c]
            us.append((u.astype(BF16),) * 2)
        mixed.append(us)

    scores = []
    for ck in range(nck):
        per_head = []
        for cols in heads:
            qd, kdiag = mixed[ck][0]
            sc = _dot_nt(qd[:, cols], kdiag[:, cols])
            for li in range(len(_HG_LEVELS)):
                u = mixed[ck][len(_HG_LEVELS) - li][0][:, cols]
                sc = jnp.where(level_of[li], _dot_nt(u, u), sc)
            per_head.append(jnp.where(causal, sc, 0.0).astype(BF16))
        scores.append(per_head)

    for ck in range(nck):
        rows = slice(ck * c, (ck + 1) * c)
        v = i_ref[rows, :]
        gate = _silu(g_ref[rows, :].astype(F32))
        for h, cols in enumerate(heads):
            st = st_ref[h]
            o = _dot_nt(qe[ck][:, cols], st.astype(BF16)) + _dot(scores[ck][h], v[:, cols])
            st_ref[h] = st * e_last[ck][:, cols] + _dot_tn(v[:, cols], kd[ck][:, cols])
            o = o * _rms_scale(o) * og
            y_ref[rows, cols] = (o * gate[:, cols]).astype(y_ref.dtype)


def _hgrn2(q, f, i, g, lb_logits, o_gain, layer, batch, seq, ts):
    tokens = q.shape[0]
    per_b = seq // ts
    em = jnp.asarray(np.tile(_hgrn2_decay_matrix(), (1, 3)), dtype=BF16)
    tile = pl.BlockSpec((ts, HG_W), lambda b, s: (b * per_b + s, 0))
    return pl.pallas_call(
        functools.partial(_hgrn2_kernel, layer=layer),
        grid=(batch, per_b),
        in_specs=[
            tile, tile, tile, tile,
            pl.BlockSpec(lb_logits.shape, lambda b, s: (0, 0)),
            pl.BlockSpec((1, HG_DV), lambda b, s: (0, 0)),
            pl.BlockSpec(em.shape, lambda b, s: (0, 0)),
        ],
        out_specs=tile,
        out_shape=jax.ShapeDtypeStruct((tokens, HG_W), BF16),
        scratch_shapes=[pltpu.VMEM((HG_HEADS, HG_DV, HG_DK), F32)],
        compiler_params=pltpu.CompilerParams(dimension_semantics=("parallel", "arbitrary")),
        name="hgrn2",
    )(q, f, i, g, lb_logits, o_gain, em)


_BIAS_TERMS = 3
_PAIRS_PER_STEP = 2


def _pair_norm(x, gain):
    r = lax.broadcasted_iota(jnp.int32, (LANES, LANES), 0)
    c = lax.broadcasted_iota(jnp.int32, (LANES, LANES), 1)
    same_head = (r // FOX_DH == c // FOX_DH).astype(BF16)
    sq = x * x
    hi = sq.astype(BF16)
    lo = (sq - hi.astype(F32)).astype(BF16)
    ssum = _dot(hi, same_head) + _dot(lo, same_head)
    return x * lax.rsqrt(ssum * (1.0 / FOX_DH) + EPS) * gain


def _fox_attn_kernel(qt_ref, k_ref, vt_ref, cs_ref, qg_ref, kg_ref, o_ref,
                     kaug_ref, vaug_ref, acc_ref, s_ref, m_ref, *, tq, tk):
    seq = k_ref.shape[1]
    i = pl.program_id(2)

    @pl.when(i == 0)
    def _():
        kg = kg_ref[...]
        r = lax.broadcasted_iota(jnp.int32, (LANES, LANES), 0)
        c = lax.broadcasted_iota(jnp.int32, (LANES, LANES), 1)
        ones = jnp.ones((FOX_DH, 512), BF16)
        for u in range(_PAIRS_PER_STEP):
            pair = _PAIRS_PER_STEP * pl.program_id(1) + u
            src = FOX_HEADS * (c % _BIAS_TERMS) + 2 * pair + c // _BIAS_TERMS
            pick = ((r == src) & (c < 2 * _BIAS_TERMS)).astype(BF16)
            f0 = u * LANES
            for r0 in range(0, seq, 512):
                rows = slice(r0, r0 + 512)
                kn = _pair_norm(k_ref[0, rows, f0:f0 + LANES].astype(F32), kg)
                kaug_ref[u, rows, 0:LANES] = kn.astype(BF16)
                kaug_ref[u, rows, LANES:2 * LANES] = _dot(cs_ref[0, rows, :], pick).astype(BF16)
                vaug_ref[2 * u, 0:FOX_DH, rows] = vt_ref[0, f0:f0 + FOX_DH, rows]
                vaug_ref[2 * u, FOX_DH:LANES, rows] = ones
                vaug_ref[2 * u + 1, 0:FOX_DH, rows] = ones
                vaug_ref[2 * u + 1, FOX_DH:LANES, rows] = vt_ref[0, f0 + FOX_DH:f0 + LANES, rows]

    def head_norm(xh):
        return xh * lax.rsqrt(jnp.mean(xh * xh, axis=0, keepdims=True) + EPS)

    zeros = jnp.zeros((FOX_DH, tq), F32)
    ri = lax.broadcasted_iota(jnp.int32, (LANES, tq), 0)
    qa = []
    for u in range(_PAIRS_PER_STEP):
        qt = qt_ref[0, u * LANES:(u + 1) * LANES, :].astype(F32)
        qn = jnp.concatenate([head_norm(qt[0:FOX_DH]), head_norm(qt[FOX_DH:LANES])], axis=0)
        qn = qn * qg_ref[...] * (FOX_DH ** -0.5 * LOG2E)
        for hh in range(2):
            feats = [qn[0:FOX_DH], zeros] if hh == 0 else [zeros, qn[FOX_DH:LANES]]
            ind = ((ri >= _BIAS_TERMS * hh) & (ri < _BIAS_TERMS * (hh + 1))).astype(F32)
            qa.append(jnp.concatenate(feats + [ind], axis=0).astype(BF16))
    chains = len(qa)

    acc_ref[...] = jnp.zeros_like(acc_ref)
    on_or_below = (lax.broadcasted_iota(jnp.int32, (tk, tk), 0)
                   <= lax.broadcasted_iota(jnp.int32, (tk, tk), 1))
    tri_bias = jnp.where(on_or_below, 0.0, -jnp.inf).astype(F32)

    m_ref[...] = jnp.full(m_ref.shape, -jnp.inf, F32)

    def put_scores(kb, slot):
        k0 = pl.multiple_of(kb * tk, tk)
        for u in range(_PAIRS_PER_STEP):
            ka = kaug_ref[u, pl.ds(k0, tk), :]
            for hh in range(2):
                s_ref[chains * slot + 2 * u + hh] = _dot(ka, qa[2 * u + hh])

    def update(ch, s, k0, c0):
        cols = slice(c0, c0 + s.shape[1])
        m_old = m_ref[ch, :, cols]
        m_new = jnp.maximum(m_old, jnp.max(s, axis=0, keepdims=True))
        m_ref[ch, :, cols] = m_new
        alpha = jnp.exp2(m_old - m_new)
        pt = jnp.exp2(s - m_new).astype(BF16)
        acc_ref[ch, :, cols] = (
            alpha * acc_ref[ch, :, cols] + _dot(vaug_ref[ch, :, pl.ds(k0, tk)], pt))

    def consume(kb, slot):
        k0 = pl.multiple_of(kb * tk, tk)
        for ch in range(chains):
            update(ch, s_ref[chains * slot + ch], k0, 0)

    def diagonal():
        k0 = pl.multiple_of(2 * i * tk, tk)
        k1 = pl.multiple_of((2 * i + 1) * tk, tk)
        late = []
        for u in range(_PAIRS_PER_STEP):
            ka = kaug_ref[u, pl.ds(k1, tk), :]
            for hh in range(2):
                late.append(_dot(ka, qa[2 * u + hh][:, tk:tq]) + tri_bias)
        for ch in range(chains):
            s = s_ref[ch]
            update(ch, jnp.concatenate([s[:, 0:tk] + tri_bias, s[:, tk:tq]], axis=1), k0, 0)
        for ch in range(chains):
            update(ch, late[ch], k1, tk)

    def two_blocks(j, carry):
        kb = 2 * j
        put_scores(kb + 1, 1)
        consume(kb, 0)
        put_scores(kb + 2, 0)
        consume(kb + 1, 1)
        return carry

    def four_blocks(jj, carry):
        two_blocks(2 * jj, carry)
        return two_blocks(2 * jj + 1, carry)

    put_scores(0, 0)
    lax.fori_loop(0, i // 2, four_blocks, 0)

    @pl.when(i % 2 == 1)
    def _():
        two_blocks(i - 1, 0)

    diagonal()

    for u in range(_PAIRS_PER_STEP):
        a0 = acc_ref[2 * u]
        a1 = acc_ref[2 * u + 1]
        out_t = jnp.concatenate(
            [a0[0:FOX_DH] / a0[FOX_DH:FOX_DH + 1], a1[FOX_DH:LANES] / a1[0:1]], axis=0)
        o_ref[0, :, u * LANES:(u + 1) * LANES] = out_t.T.astype(o_ref.dtype)


def _fox_attn(qt, k, vt, cs, q_gain_cols, k_gain2, tq):
    batch, seq, _ = k.shape
    tk = tq // 2
    width = _PAIRS_PER_STEP * LANES
    chains = 2 * _PAIRS_PER_STEP
    kernel = functools.partial(_fox_attn_kernel, tq=tq, tk=tk)
    return pl.pallas_call(
        kernel,
        grid=(batch, FOX_W // width, seq // tq),
        in_specs=[
            pl.BlockSpec((1, width, tq), lambda b, p, i: (b, p, i)),
            pl.BlockSpec((1, seq, width), lambda b, p, i: (b, 0, p)),
            pl.BlockSpec((1, width, seq), lambda b, p, i: (b, p, 0)),
            pl.BlockSpec((1, seq, LANES), lambda b, p, i: (b, 0, 0)),
            pl.BlockSpec((LANES, tq), lambda b, p, i: (0, 0)),
            pl.BlockSpec((1, LANES), lambda b, p, i: (0, 0)),
        ],
        out_specs=pl.BlockSpec((1, tq, width), lambda b, p, i: (b, i, p)),
        out_shape=jax.ShapeDtypeStruct((batch, seq, FOX_W), BF16),
        scratch_shapes=[
            pltpu.VMEM((_PAIRS_PER_STEP, seq, 2 * LANES), BF16),
            pltpu.VMEM((chains, LANES, seq), BF16),
            pltpu.VMEM((chains, LANES, tq), F32),
            pltpu.VMEM((2 * chains, tk, tq), F32),
            pltpu.VMEM((chains, 1, tq), F32),
        ],
        compiler_params=pltpu.CompilerParams(
            dimension_semantics=("parallel", "parallel", "arbitrary"),
            vmem_limit_bytes=VMEM_LIMIT_BYTES),
        name="fox_attn",
    )(qt, k, vt, cs, q_gain_cols, k_gain2)


def _ff_chunks(d_ff):
    chunks, off = [], 0
    while off < d_ff:
        wd = min(1024, d_ff - off)
        chunks.append((off, wd))
        off += wd
    return chunks


def _tail_kernel(x_ref, ya_ref, yb_ref, ga_ref, gb_ref, p_ref, wa_ref, wb_ref, wo_ref,
                 gf_ref, wg_ref, wu_ref, wd_ref, gp_ref, wpg_ref, wpp_ref, o_ref):
    ga = _sigmoid(ga_ref[...].astype(F32))
    gb = _sigmoid(gb_ref[...].astype(F32))
    merged = ga * _dot(ya_ref[...], wa_ref[...]) + gb * _dot(yb_ref[...], wb_ref[...])
    x = x_ref[...] + _dot(merged.astype(BF16), wo_ref[...])

    hf = (x * _rms_scale(x) * gf_ref[...]).astype(BF16)
    ffn = None
    for off, wd in _ff_chunks(wg_ref.shape[1]):
        act = _silu(_dot(hf, wg_ref[:, off:off + wd])) * _dot(hf, wu_ref[:, off:off + wd])
        part = _dot(act.astype(BF16), wd_ref[off:off + wd, :])
        ffn = part if ffn is None else ffn + part
    x = x + ffn

    hp = (x * _rms_scale(x) * gp_ref[...]).astype(BF16)
    emb = _dot(p_ref[...].astype(BF16), wpp_ref[...])
    o_ref[...] = x + _sigmoid(_dot(hp, wpg_ref[...])) * emb


def _tail(x2d, ya, yb, ga, gb, p2d, wa, wb, wo, gf, wg, wu, wd, gp, wpg, wpp, tm):
    tokens, d = x2d.shape

    def tile(width):
        return pl.BlockSpec((tm, width), lambda i: (i, 0))

    def whole(arr):
        return pl.BlockSpec(arr.shape, lambda i: (0, 0), pipeline_mode=pl.Buffered(1))

    return pl.pallas_call(
        _tail_kernel,
        grid=(tokens // tm,),
        in_specs=[
            tile(d), tile(ya.shape[1]), tile(yb.shape[1]), tile(d), tile(d), tile(p2d.shape[1]),
            whole(wa), whole(wb), whole(wo), whole(gf), whole(wg), whole(wu), whole(wd),
            whole(gp), whole(wpg), whole(wpp),
        ],
        out_specs=tile(d),
        out_shape=jax.ShapeDtypeStruct((tokens, d), F32),
        compiler_params=pltpu.CompilerParams(
            dimension_semantics=("parallel",), vmem_limit_bytes=VMEM_LIMIT_BYTES),
        name="tail",
    )(x2d, ya, yb, ga, gb, p2d, wa, wb, wo, gf, wg, wu, wd, gp, wpg, wpp)


def _tile(n, pref):
    return pref if n % pref == 0 else n


def kernel(x, p, norm_mix_g, w_in, hg_lb_logits, hg_onorm_g, fox_f_bias, fox_q_norm_g,
           fox_k_norm_g, w_branch_a, w_branch_b, w_out, norm_ffn_g, w_ffn_gate, w_ffn_up,
           w_ffn_down, norm_ple_g, w_ple_gate, w_ple_proj):
    batch, seq, d = x.shape
    depth = p.shape[0]
    off_fox_q = 4 * HG_W
    off_fox_k = off_fox_q + FOX_W
    off_fox_v = off_fox_k + FOX_W
    off_fox_f = off_fox_v + FOX_W
    off_gate = off_fox_f + FOX_HEADS
    assert w_in.shape[2] == off_gate + 2 * d and d == 1024
    assert seq % 512 == 0

    tm = 512
    ts = 512
    tq = 512

    x2d = x.reshape(batch * seq, d)
    for layer in range(depth):
        wl = w_in[layer]
        w_main = jnp.concatenate(
            [wl[:, :off_fox_q], wl[:, off_fox_k:off_fox_v], wl[:, off_gate:],
             jnp.pad(wl[:, off_fox_f:off_gate], ((0, 0), (0, LANES - FOX_HEADS)))],
            axis=1).astype(BF16)
        w_t = jnp.concatenate(
            [wl[:, off_fox_q:off_fox_k], wl[:, off_fox_v:off_fox_f]], axis=1).T.astype(BF16)
        (hg_q, hg_f, hg_i, hg_g, fox_k, gate_a, gate_b, fox_f, fox_qt, fox_vt) = _in_proj(
            x2d, norm_mix_g[layer][None, :], w_main, w_t, batch, seq, tm)

        bias_row = jnp.pad(fox_f_bias[layer].astype(F32), (0, LANES - FOX_HEADS))[None, :]
        cs = _fox_decay(fox_f.reshape(batch, seq, LANES), bias_row)

        y_a = _hgrn2(hg_q, hg_f, hg_i, hg_g, hg_lb_logits.astype(F32),
                     hg_onorm_g[layer].astype(F32)[None, :], layer, batch, seq, ts)

        q_gain_cols = jnp.broadcast_to(
            jnp.tile(fox_q_norm_g[layer].astype(F32), 2)[:, None], (LANES, tq))
        y_b = _fox_attn(fox_qt, fox_k.reshape(batch, seq, FOX_W), fox_vt, cs, q_gain_cols,
                        jnp.tile(fox_k_norm_g[layer].astype(F32), 2)[None, :], tq)

        x2d = _tail(
            x2d, y_a, y_b.reshape(batch * seq, FOX_W), gate_a, gate_b,
            p[layer].reshape(batch * seq, -1),
            w_branch_a[layer].astype(BF16), w_branch_b[layer].astype(BF16),
            w_out[layer].astype(BF16), norm_ffn_g[layer][None, :],
            w_ffn_gate[layer].astype(BF16), w_ffn_up[layer].astype(BF16),
            w_ffn_down[layer].astype(BF16), norm_ple_g[layer][None, :],
            w_ple_gate[layer].astype(BF16), w_ple_proj[layer].astype(BF16), tm)
    return x2d.reshape(batch, seq, d)
```

```python
import functools

import numpy as np
import jax
import jax.numpy as jnp
from jax import lax
from jax.experimental import pallas as pl
from jax.experimental.pallas import tpu as pltpu

F32 = jnp.float32
BF16 = jnp.bfloat16
EPS = 1e-6
LOG2E = 1.4426950408889634

HG_HEADS = 4
HG_DK = 128
HG_DV = 128
HG_W = HG_HEADS * HG_DK
HG_CHUNK = 64
FOX_HEADS = 8
FOX_DH = 64
FOX_W = FOX_HEADS * FOX_DH
D_MODEL = 1024
LANES = 128
MXU_WIDTH = 256
VMEM_LIMIT_BYTES = 56 * 1024 * 1024

TOKEN_TILE = 512
HG_TILE = 8 * HG_CHUNK
FOX_QUERY_TILE = 2 * MXU_WIDTH
FOX_PREP_ROWS = 512


def _dot(a, b):
    return jnp.dot(a, b, preferred_element_type=F32)


def _dot_nt(a, b):
    return lax.dot_general(a, b, (((1,), (1,)), ((), ())), preferred_element_type=F32)


def _dot_tn(a, b):
    return lax.dot_general(a, b, (((0,), (0,)), ((), ())), preferred_element_type=F32)


def _split3(a):
    hi = a.astype(BF16)
    r1 = a - hi.astype(F32)
    mid = r1.astype(BF16)
    lo = (r1 - mid.astype(F32)).astype(BF16)
    return hi, mid, lo


def _dot_exact_rhs(m01, a):
    hi, mid, lo = _split3(a)
    return _dot(m01, hi) + _dot(m01, mid) + _dot(m01, lo)


def _sigmoid(x):
    return 1.0 / (1.0 + jnp.exp2(x * (-LOG2E)))


def _silu(x):
    return x * _sigmoid(x)


def _rms_scale(x):
    return lax.rsqrt(jnp.mean(x * x, axis=-1, keepdims=True) + EPS)


_IN_SEGS = (
    ("hg_q", HG_W, BF16),
    ("hg_f", HG_W, F32),
    ("hg_i", HG_W, BF16),
    ("hg_g", HG_W, BF16),
    ("fox_k", FOX_W, BF16),
    ("gate_a", D_MODEL, BF16),
    ("gate_b", D_MODEL, BF16),
    ("fox_f", LANES, F32),
)


def _in_proj_kernel(x_ref, g_ref, w_ref, wt_ref, *out_refs):
    x = x_ref[...]
    h = (x * _rms_scale(x) * g_ref[...]).astype(BF16)
    off = 0
    for (_, width, dtype), o_ref in zip(_IN_SEGS, out_refs[:-2]):
        o_ref[...] = _dot(h, w_ref[:, off:off + width]).astype(dtype)
        off += width
    out_refs[-2][0] = _dot_nt(wt_ref[0:FOX_W, :], h).astype(BF16)
    out_refs[-1][0] = _dot_nt(wt_ref[FOX_W:2 * FOX_W, :], h).astype(BF16)


def _in_proj(x2d, g, w_main, w_t, batch, seq, tm):
    tokens, d = x2d.shape
    per_b = seq // tm
    out_shape = [jax.ShapeDtypeStruct((tokens, wd), dt) for _, wd, dt in _IN_SEGS]
    out_specs = [pl.BlockSpec((tm, wd), lambda i: (i, 0)) for _, wd, _ in _IN_SEGS]
    for _ in range(2):
        out_shape.append(jax.ShapeDtypeStruct((batch, FOX_W, seq), BF16))
        out_specs.append(pl.BlockSpec((1, FOX_W, tm), lambda i: (i // per_b, 0, i % per_b)))
    return pl.pallas_call(
        _in_proj_kernel,
        grid=(tokens // tm,),
        in_specs=[
            pl.BlockSpec((tm, d), lambda i: (i, 0)),
            pl.BlockSpec((1, d), lambda i: (0, 0)),
            pl.BlockSpec(w_main.shape, lambda i: (0, 0), pipeline_mode=pl.Buffered(1)),
            pl.BlockSpec(w_t.shape, lambda i: (0, 0), pipeline_mode=pl.Buffered(1)),
        ],
        out_specs=out_specs,
        out_shape=out_shape,
        compiler_params=pltpu.CompilerParams(
            dimension_semantics=("parallel",), vmem_limit_bytes=VMEM_LIMIT_BYTES),
        name="in_proj",
    )(x2d, g, w_main, w_t)


def _fox_decay_kernel(f_ref, bias_ref, cs_ref):
    seq = f_ref.shape[1]
    r = lax.broadcasted_iota(jnp.int32, (LANES, LANES), 0)
    c = lax.broadcasted_iota(jnp.int32, (LANES, LANES), 1)
    lower = (c <= r).astype(BF16)
    place = [((c == r + FOX_HEADS * j) & (r < FOX_HEADS)).astype(BF16) for j in range(3)]
    bias = bias_ref[...]
    carry = jnp.zeros((1, LANES), F32)
    for j in range(seq // LANES):
        rows = slice(j * LANES, (j + 1) * LANES)
        z = f_ref[0, rows, :] + bias
        ls = jnp.minimum(z, 0.0) - jnp.log1p(jnp.exp(-jnp.abs(z)))
        cj = _dot_exact_rhs(lower, ls) + carry
        carry = cj[LANES - 1:LANES, :]
        hi, mid, lo = _split3(cj * (-LOG2E))
        cs_ref[0, rows, :] = (
            _dot(hi, place[0]) + _dot(mid, place[1]) + _dot(lo, place[2])).astype(BF16)


def _fox_decay(f_logits, bias_row):
    batch, seq, _ = f_logits.shape
    return pl.pallas_call(
        _fox_decay_kernel,
        grid=(batch,),
        in_specs=[
            pl.BlockSpec((1, seq, LANES), lambda b: (b, 0, 0)),
            pl.BlockSpec((1, LANES), lambda b: (0, 0)),
        ],
        out_specs=pl.BlockSpec((1, seq, LANES), lambda b: (b, 0, 0)),
        out_shape=jax.ShapeDtypeStruct((batch, seq, LANES), BF16),
        compiler_params=pltpu.CompilerParams(dimension_semantics=("parallel",)),
        name="fox_decay",
    )(f_logits, bias_row)


_HG_LEVELS = (32, 16, 8, 4, 2, 1)


def _hgrn2_decay_matrix():
    c = HG_CHUNK
    t = np.arange(c)[:, None]
    j = np.arange(c)[None, :]
    groups = [(j <= t), (j > t)]
    for n in _HG_LEVELS:
        ref = (t // (2 * n)) * (2 * n) + n - 1
        second = (t & n) != 0
        groups.append(np.where(second, (j > ref) & (j <= t), (j > t) & (j <= ref)))
    return np.concatenate(groups, axis=0).astype(np.float32)


def _hgrn2_kernel(q_ref, f_ref, i_ref, g_ref, lb_ref, og_ref, em_ref, y_ref, st_ref, *, layer):
    ts = q_ref.shape[0]
    c = HG_CHUNK

    @pl.when(pl.program_id(1) == 0)
    def _():
        st_ref[...] = jnp.zeros_like(st_ref)

    lbl = lb_ref[...]
    lbe = jnp.exp(lbl - jnp.max(lbl, axis=0, keepdims=True))
    lb = jnp.sum(lbe[0:layer + 1, :], axis=0, keepdims=True) / jnp.sum(lbe, axis=0, keepdims=True)

    em3 = em_ref[...]
    og = og_ref[...]
    row = lax.broadcasted_iota(jnp.int32, (c, HG_W), 0)
    sublanes = 8

    def pick_side(n, qf, kf):
        if n >= sublanes:
            return jnp.concatenate(
                [(qf if (r0 // n) % 2 else kf)[r0:r0 + n] for r0 in range(0, c, n)], axis=0)
        return jnp.where((row & n) != 0, qf, kf)

    tt = lax.broadcasted_iota(jnp.int32, (c, c), 0)
    ss = lax.broadcasted_iota(jnp.int32, (c, c), 1)
    txs = tt ^ ss
    level_of = [txs >= n for n in reversed(_HG_LEVELS)]
    causal = tt >= ss
    nck = ts // c
    heads = [slice(h * HG_DK, (h + 1) * HG_DK) for h in range(HG_HEADS)]

    qe, kd, e_last, mixed = [], [], [], []
    for ck in range(nck):
        rows = slice(ck * c, (ck + 1) * c)
        qf = _silu(q_ref[rows, :].astype(F32))
        fg = lb + (1.0 - lb) * _sigmoid(f_ref[rows, :])
        kf = 1.0 - fg
        terms = jnp.concatenate(_split3(jnp.log2(fg)), axis=0)
        decay = jnp.exp2(_dot(em3, terms))
        qe.append((qf * decay[0:c]).astype(BF16))
        kd.append((kf * decay[c:2 * c]).astype(BF16))
        e_last.append(decay[c - 1:c, :])
        us = [(qf.astype(BF16), kf.astype(BF16))]
        for li, n in enumerate(_HG_LEVELS):
            u = pick_side(n, qf, kf) * decay[(2 + li) * c:(3 + li) * c]
            us.append((u.astype(BF16),) * 2)
        mixed.append(us)

    scores = []
    for ck in range(nck):
        per_head = []
        for cols in heads:
            qd, kdiag = mixed[ck][0]
            sc = _dot_nt(qd[:, cols], kdiag[:, cols])
            for li in range(len(_HG_LEVELS)):
                u = mixed[ck][len(_HG_LEVELS) - li][0][:, cols]
                sc = jnp.where(level_of[li], _dot_nt(u, u), sc)
            per_head.append(jnp.where(causal, sc, 0.0).astype(BF16))
        scores.append(per_head)

    for ck in range(nck):
        rows = slice(ck * c, (ck + 1) * c)
        v = i_ref[rows, :]
        gate = _silu(g_ref[rows, :].astype(F32))
        for h, cols in enumerate(heads):
            st = st_ref[h]
            o = _dot_nt(qe[ck][:, cols], st.astype(BF16)) + _dot(scores[ck][h], v[:, cols])
            st_ref[h] = st * e_last[ck][:, cols] + _dot_tn(v[:, cols], kd[ck][:, cols])
            o = o * _rms_scale(o) * og
            y_ref[rows, cols] = (o * gate[:, cols]).astype(y_ref.dtype)


def _hgrn2(q, f, i, g, lb_logits, o_gain, layer, batch, seq, ts):
    tokens = q.shape[0]
    per_b = seq // ts
    em = jnp.asarray(np.tile(_hgrn2_decay_matrix(), (1, 3)), dtype=BF16)
    tile = pl.BlockSpec((ts, HG_W), lambda b, s: (b * per_b + s, 0))
    return pl.pallas_call(
        functools.partial(_hgrn2_kernel, layer=layer),
        grid=(batch, per_b),
        in_specs=[
            tile, tile, tile, tile,
            pl.BlockSpec(lb_logits.shape, lambda b, s: (0, 0)),
            pl.BlockSpec((1, HG_DV), lambda b, s: (0, 0)),
            pl.BlockSpec(em.shape, lambda b, s: (0, 0)),
        ],
        out_specs=tile,
        out_shape=jax.ShapeDtypeStruct((tokens, HG_W), BF16),
        scratch_shapes=[pltpu.VMEM((HG_HEADS, HG_DV, HG_DK), F32)],
        compiler_params=pltpu.CompilerParams(dimension_semantics=("parallel", "arbitrary")),
        name="hgrn2",
    )(q, f, i, g, lb_logits, o_gain, em)


_BIAS_TERMS = 3
_PAIRS_PER_STEP = 2


def _pair_norm(x, gain):
    r = lax.broadcasted_iota(jnp.int32, (LANES, LANES), 0)
    c = lax.broadcasted_iota(jnp.int32, (LANES, LANES), 1)
    same_head = (r // FOX_DH == c // FOX_DH).astype(BF16)
    sq = x * x
    hi = sq.astype(BF16)
    lo = (sq - hi.astype(F32)).astype(BF16)
    ssum = _dot(hi, same_head) + _dot(lo, same_head)
    return x * lax.rsqrt(ssum * (1.0 / FOX_DH) + EPS) * gain


def _fox_attn_kernel(qt_ref, k_ref, vt_ref, cs_ref, qg_ref, kg_ref, o_ref,
                     kaug_ref, vaug_ref, acc_ref, s_ref, m_ref, *, tq, tk):
    seq = k_ref.shape[1]
    i = pl.program_id(2)

    @pl.when(i == 0)
    def _():
        kg = kg_ref[...]
        r = lax.broadcasted_iota(jnp.int32, (LANES, LANES), 0)
        c = lax.broadcasted_iota(jnp.int32, (LANES, LANES), 1)
        ones = jnp.ones((FOX_DH, FOX_PREP_ROWS), BF16)
        for u in range(_PAIRS_PER_STEP):
            pair = _PAIRS_PER_STEP * pl.program_id(1) + u
            src = FOX_HEADS * (c % _BIAS_TERMS) + 2 * pair + c // _BIAS_TERMS
            pick = ((r == src) & (c < 2 * _BIAS_TERMS)).astype(BF16)
            f0 = u * LANES
            for r0 in range(0, seq, FOX_PREP_ROWS):
                rows = slice(r0, r0 + FOX_PREP_ROWS)
                kn = _pair_norm(k_ref[0, rows, f0:f0 + LANES].astype(F32), kg)
                kaug_ref[u, rows, 0:LANES] = kn.astype(BF16)
                kaug_ref[u, rows, LANES:2 * LANES] = _dot(cs_ref[0, rows, :], pick).astype(BF16)
                vaug_ref[2 * u, 0:FOX_DH, rows] = vt_ref[0, f0:f0 + FOX_DH, rows]
                vaug_ref[2 * u, FOX_DH:LANES, rows] = ones
                vaug_ref[2 * u + 1, 0:FOX_DH, rows] = ones
                vaug_ref[2 * u + 1, FOX_DH:LANES, rows] = vt_ref[0, f0 + FOX_DH:f0 + LANES, rows]

    def head_norm(xh):
        return xh * lax.rsqrt(jnp.mean(xh * xh, axis=0, keepdims=True) + EPS)

    zeros = jnp.zeros((FOX_DH, tq), F32)
    ri = lax.broadcasted_iota(jnp.int32, (LANES, tq), 0)
    qa = []
    for u in range(_PAIRS_PER_STEP):
        qt = qt_ref[0, u * LANES:(u + 1) * LANES, :].astype(F32)
        qn = jnp.concatenate([head_norm(qt[0:FOX_DH]), head_norm(qt[FOX_DH:LANES])], axis=0)
        qn = qn * qg_ref[...] * (FOX_DH ** -0.5 * LOG2E)
        for hh in range(2):
            feats = [qn[0:FOX_DH], zeros] if hh == 0 else [zeros, qn[FOX_DH:LANES]]
            ind = ((ri >= _BIAS_TERMS * hh) & (ri < _BIAS_TERMS * (hh + 1))).astype(F32)
            qa.append(jnp.concatenate(feats + [ind], axis=0).astype(BF16))
    chains = len(qa)

    acc_ref[...] = jnp.zeros_like(acc_ref)
    on_or_below = (lax.broadcasted_iota(jnp.int32, (tk, tk), 0)
                   <= lax.broadcasted_iota(jnp.int32, (tk, tk), 1))
    tri_bias = jnp.where(on_or_below, 0.0, -jnp.inf).astype(F32)

    m_ref[...] = jnp.full(m_ref.shape, -jnp.inf, F32)

    def put_scores(kb, slot):
        k0 = pl.multiple_of(kb * tk, tk)
        for u in range(_PAIRS_PER_STEP):
            ka = kaug_ref[u, pl.ds(k0, tk), :]
            for hh in range(2):
                s_ref[chains * slot + 2 * u + hh] = _dot(ka, qa[2 * u + hh])

    def update(ch, s, k0, c0):
        cols = slice(c0, c0 + s.shape[1])
        m_old = m_ref[ch, :, cols]
        m_new = jnp.maximum(m_old, jnp.max(s, axis=0, keepdims=True))
        m_ref[ch, :, cols] = m_new
        alpha = jnp.exp2(m_old - m_new)
        pt = jnp.exp2(s - m_new).astype(BF16)
        acc_ref[ch, :, cols] = (
            alpha * acc_ref[ch, :, cols] + _dot(vaug_ref[ch, :, pl.ds(k0, tk)], pt))

    def consume(kb, slot):
        k0 = pl.multiple_of(kb * tk, tk)
        for ch in range(chains):
            update(ch, s_ref[chains * slot + ch], k0, 0)

    def diagonal():
        k0 = pl.multiple_of(2 * i * tk, tk)
        k1 = pl.multiple_of((2 * i + 1) * tk, tk)
        late = []
        for u in range(_PAIRS_PER_STEP):
            ka = kaug_ref[u, pl.ds(k1, tk), :]
            for hh in range(2):
                late.append(_dot(ka, qa[2 * u + hh][:, tk:tq]) + tri_bias)
        for ch in range(chains):
            s = s_ref[ch]
            update(ch, jnp.concatenate([s[:, 0:tk] + tri_bias, s[:, tk:tq]], axis=1), k0, 0)
        for ch in range(chains):
            update(ch, late[ch], k1, tk)

    def two_blocks(j, carry):
        kb = 2 * j
        put_scores(kb + 1, 1)
        consume(kb, 0)
        put_scores(kb + 2, 0)
        consume(kb + 1, 1)
        return carry

    def four_blocks(jj, carry):
        two_blocks(2 * jj, carry)
        return two_blocks(2 * jj + 1, carry)

    put_scores(0, 0)
    lax.fori_loop(0, i // 2, four_blocks, 0)

    @pl.when(i % 2 == 1)
    def _():
        two_blocks(i - 1, 0)

    diagonal()

    for u in range(_PAIRS_PER_STEP):
        a0 = acc_ref[2 * u]
        a1 = acc_ref[2 * u + 1]
        out_t = jnp.concatenate(
            [a0[0:FOX_DH] / a0[FOX_DH:FOX_DH + 1], a1[FOX_DH:LANES] / a1[0:1]], axis=0)
        o_ref[0, :, u * LANES:(u + 1) * LANES] = out_t.T.astype(o_ref.dtype)


def _fox_attn(qt, k, vt, cs, q_gain_cols, k_gain2, tq):
    batch, seq, _ = k.shape
    tk = tq // 2
    width = _PAIRS_PER_STEP * LANES
    chains = 2 * _PAIRS_PER_STEP
    kernel = functools.partial(_fox_attn_kernel, tq=tq, tk=tk)
    return pl.pallas_call(
        kernel,
        grid=(batch, FOX_W // width, seq // tq),
        in_specs=[
            pl.BlockSpec((1, width, tq), lambda b, p, i: (b, p, i)),
            pl.BlockSpec((1, seq, width), lambda b, p, i: (b, 0, p)),
            pl.BlockSpec((1, width, seq), lambda b, p, i: (b, p, 0)),
            pl.BlockSpec((1, seq, LANES), lambda b, p, i: (b, 0, 0)),
            pl.BlockSpec((LANES, tq), lambda b, p, i: (0, 0)),
            pl.BlockSpec((1, LANES), lambda b, p, i: (0, 0)),
        ],
        out_specs=pl.BlockSpec((1, tq, width), lambda b, p, i: (b, i, p)),
        out_shape=jax.ShapeDtypeStruct((batch, seq, FOX_W), BF16),
        scratch_shapes=[
            pltpu.VMEM((_PAIRS_PER_STEP, seq, 2 * LANES), BF16),
            pltpu.VMEM((chains, LANES, seq), BF16),
            pltpu.VMEM((chains, LANES, tq), F32),
            pltpu.VMEM((2 * chains, tk, tq), F32),
            pltpu.VMEM((chains, 1, tq), F32),
        ],
        compiler_params=pltpu.CompilerParams(
            dimension_semantics=("parallel", "parallel", "arbitrary"),
            vmem_limit_bytes=VMEM_LIMIT_BYTES),
        name="fox_attn",
    )(qt, k, vt, cs, q_gain_cols, k_gain2)


_TAIL_ROW_GROUPS = 2


def _ff_chunks(d_ff):
    chunks, off = [], 0
    while off < d_ff:
        wd = min(4 * MXU_WIDTH, d_ff - off)
        chunks.append((off, wd))
        off += wd
    return chunks


def _tail_kernel(x_ref, ya_ref, yb_ref, ga_ref, gb_ref, p_ref, wa_ref, wb_ref, wo_ref,
                 gf_ref, wg_ref, wu_ref, wd_ref, gp_ref, wpg_ref, wpp_ref, o_ref):
    tm = x_ref.shape[0]
    groups = [slice(r0, r0 + tm // _TAIL_ROW_GROUPS) for r0 in range(0, tm, tm // _TAIL_ROW_GROUPS)]

    xs = []
    for rows in groups:
        ga = _sigmoid(ga_ref[rows, :].astype(F32))
        gb = _sigmoid(gb_ref[rows, :].astype(F32))
        merged = (ga * _dot(ya_ref[rows, :], wa_ref[...])
                  + gb * _dot(yb_ref[rows, :], wb_ref[...]))
        xs.append(x_ref[rows, :] + _dot(merged.astype(BF16), wo_ref[...]))

    hfs = [(x * _rms_scale(x) * gf_ref[...]).astype(BF16) for x in xs]
    ffns = [None] * len(groups)
    for off, wd in _ff_chunks(wg_ref.shape[1]):
        for gi, hf in enumerate(hfs):
            act = _silu(_dot(hf, wg_ref[:, off:off + wd])) * _dot(hf, wu_ref[:, off:off + wd])
            part = _dot(act.astype(BF16), wd_ref[off:off + wd, :])
            ffns[gi] = part if ffns[gi] is None else ffns[gi] + part
    xs = [x + ffn for x, ffn in zip(xs, ffns)]

    hps = [(x * _rms_scale(x) * gp_ref[...]).astype(BF16) for x in xs]
    for rows, x, hp in zip(groups, xs, hps):
        emb = _dot(p_ref[rows, :].astype(BF16), wpp_ref[...])
        o_ref[rows, :] = x + _sigmoid(_dot(hp, wpg_ref[...])) * emb


def _tail(x2d, ya, yb, ga, gb, p2d, wa, wb, wo, gf, wg, wu, wd, gp, wpg, wpp, tm):
    tokens, d = x2d.shape

    def tile(width):
        return pl.BlockSpec((tm, width), lambda i: (i, 0))

    def whole(arr):
        return pl.BlockSpec(arr.shape, lambda i: (0, 0), pipeline_mode=pl.Buffered(1))

    return pl.pallas_call(
        _tail_kernel,
        grid=(tokens // tm,),
        in_specs=[
            tile(d), tile(ya.shape[1]), tile(yb.shape[1]), tile(d), tile(d), tile(p2d.shape[1]),
            whole(wa), whole(wb), whole(wo), whole(gf), whole(wg), whole(wu), whole(wd),
            whole(gp), whole(wpg), whole(wpp),
        ],
        out_specs=tile(d),
        out_shape=jax.ShapeDtypeStruct((tokens, d), F32),
        compiler_params=pltpu.CompilerParams(
            dimension_semantics=("parallel",), vmem_limit_bytes=VMEM_LIMIT_BYTES),
        name="tail",
    )(x2d, ya, yb, ga, gb, p2d, wa, wb, wo, gf, wg, wu, wd, gp, wpg, wpp)


def kernel(x, p, norm_mix_g, w_in, hg_lb_logits, hg_onorm_g, fox_f_bias, fox_q_norm_g,
           fox_k_norm_g, w_branch_a, w_branch_b, w_out, norm_ffn_g, w_ffn_gate, w_ffn_up,
           w_ffn_down, norm_ple_g, w_ple_gate, w_ple_proj):
    batch, seq, d = x.shape
    depth = p.shape[0]
    off_fox_q = 4 * HG_W
    off_fox_k = off_fox_q + FOX_W
    off_fox_v = off_fox_k + FOX_W
    off_fox_f = off_fox_v + FOX_W
    off_gate = off_fox_f + FOX_HEADS
    assert w_in.shape[2] == off_gate + 2 * d and d == D_MODEL
    assert seq % TOKEN_TILE == 0 and seq % HG_TILE == 0 and seq % FOX_QUERY_TILE == 0
    assert seq % FOX_PREP_ROWS == 0

    tm = TOKEN_TILE
    ts = HG_TILE
    tq = FOX_QUERY_TILE

    x2d = x.reshape(batch * seq, d)
    for layer in range(depth):
        wl = w_in[layer]
        w_main = jnp.concatenate(
            [wl[:, :off_fox_q], wl[:, off_fox_k:off_fox_v], wl[:, off_gate:],
             jnp.pad(wl[:, off_fox_f:off_gate], ((0, 0), (0, LANES - FOX_HEADS)))],
            axis=1).astype(BF16)
        w_t = jnp.concatenate(
            [wl[:, off_fox_q:off_fox_k], wl[:, off_fox_v:off_fox_f]], axis=1).T.astype(BF16)
        (hg_q, hg_f, hg_i, hg_g, fox_k, gate_a, gate_b, fox_f, fox_qt, fox_vt) = _in_proj(
            x2d, norm_mix_g[layer][None, :], w_main, w_t, batch, seq, tm)

        bias_row = jnp.pad(fox_f_bias[layer].astype(F32), (0, LANES - FOX_HEADS))[None, :]
        cs = _fox_decay(fox_f.reshape(batch, seq, LANES), bias_row)

        y_a = _hgrn2(hg_q, hg_f, hg_i, hg_g, hg_lb_logits.astype(F32),
                     hg_onorm_g[layer].astype(F32)[None, :], layer, batch, seq, ts)

        q_gain_cols = jnp.broadcast_to(
            jnp.tile(fox_q_norm_g[layer].astype(F32), 2)[:, None], (LANES, tq))
        y_b = _fox_attn(fox_qt, fox_k.reshape(batch, seq, FOX_W), fox_vt, cs, q_gain_cols,
                        jnp.tile(fox_k_norm_g[layer].astype(F32), 2)[None, :], tq)

        x2d = _tail(
            x2d, y_a, y_b.reshape(batch * seq, FOX_W), gate_a, gate_b,
            p[layer].reshape(batch * seq, -1),
            w_branch_a[layer].astype(BF16), w_branch_b[layer].astype(BF16),
            w_out[layer].astype(BF16), norm_ffn_g[layer][None, :],
            w_ffn_gate[layer].astype(BF16), w_ffn_up[layer].astype(BF16),
            w_ffn_down[layer].astype(BF16), norm_ple_g[layer][None, :],
            w_ple_gate[layer].astype(BF16), w_ple_proj[layer].astype(BF16), tm)
    return x2d.reshape(batch, seq, d)
```

```python
import functools

import numpy as np
import jax
import jax.numpy as jnp
from jax import lax
from jax.experimental import pallas as pl
from jax.experimental.pallas import tpu as pltpu

F32 = jnp.float32
BF16 = jnp.bfloat16
EPS = 1e-6
LOG2E = 1.4426950408889634

HG_HEADS = 4
HG_DK = 128
HG_DV = 128
HG_W = HG_HEADS * HG_DK
HG_CHUNK = 64
FOX_HEADS = 8
FOX_DH = 64
FOX_W = FOX_HEADS * FOX_DH
D_MODEL = 1024
LANES = 128
MXU_WIDTH = 256
VMEM_LIMIT_BYTES = 56 * 1024 * 1024

TOKEN_TILE = 512
HG_TILE = 8 * HG_CHUNK
FOX_QUERY_TILE = 2 * MXU_WIDTH
FOX_PREP_ROWS = 512


def _dot(a, b):
    return jnp.dot(a, b, preferred_element_type=F32)


def _dot_nt(a, b):
    return lax.dot_general(a, b, (((1,), (1,)), ((), ())), preferred_element_type=F32)


def _dot_tn(a, b):
    return lax.dot_general(a, b, (((0,), (0,)), ((), ())), preferred_element_type=F32)


def _split3(a):
    hi = a.astype(BF16)
    r1 = a - hi.astype(F32)
    mid = r1.astype(BF16)
    lo = (r1 - mid.astype(F32)).astype(BF16)
    return hi, mid, lo


def _dot_exact_rhs(m01, a):
    hi, mid, lo = _split3(a)
    return _dot(m01, hi) + _dot(m01, mid) + _dot(m01, lo)


def _sigmoid(x):
    return 1.0 / (1.0 + jnp.exp2(x * (-LOG2E)))


def _silu(x):
    return x * _sigmoid(x)


def _rms_scale(x):
    return lax.rsqrt(jnp.mean(x * x, axis=-1, keepdims=True) + EPS)


_IN_SEGS = (
    ("hg_q", HG_W, BF16),
    ("hg_f", HG_W, F32),
    ("hg_i", HG_W, BF16),
    ("hg_g", HG_W, BF16),
    ("fox_k", FOX_W, BF16),
    ("gate_a", D_MODEL, BF16),
    ("gate_b", D_MODEL, BF16),
    ("fox_f", LANES, F32),
)


def _in_proj_kernel(x_ref, g_ref, w_ref, wt_ref, *out_refs):
    x = x_ref[...]
    h = (x * _rms_scale(x) * g_ref[...]).astype(BF16)
    off = 0
    for (_, width, dtype), o_ref in zip(_IN_SEGS, out_refs[:-2]):
        o_ref[...] = _dot(h, w_ref[:, off:off + width]).astype(dtype)
        off += width
    out_refs[-2][0] = _dot_nt(wt_ref[0:FOX_W, :], h).astype(BF16)
    out_refs[-1][0] = _dot_nt(wt_ref[FOX_W:2 * FOX_W, :], h).astype(BF16)


def _in_proj(x2d, g, w_main, w_t, batch, seq, tm):
    tokens, d = x2d.shape
    per_b = seq // tm
    out_shape = [jax.ShapeDtypeStruct((tokens, wd), dt) for _, wd, dt in _IN_SEGS]
    out_specs = [pl.BlockSpec((tm, wd), lambda i: (i, 0)) for _, wd, _ in _IN_SEGS]
    for _ in range(2):
        out_shape.append(jax.ShapeDtypeStruct((batch, FOX_W, seq), BF16))
        out_specs.append(pl.BlockSpec((1, FOX_W, tm), lambda i: (i // per_b, 0, i % per_b)))
    return pl.pallas_call(
        _in_proj_kernel,
        grid=(tokens // tm,),
        in_specs=[
            pl.BlockSpec((tm, d), lambda i: (i, 0)),
            pl.BlockSpec((1, d), lambda i: (0, 0)),
            pl.BlockSpec(w_main.shape, lambda i: (0, 0), pipeline_mode=pl.Buffered(1)),
            pl.BlockSpec(w_t.shape, lambda i: (0, 0), pipeline_mode=pl.Buffered(1)),
        ],
        out_specs=out_specs,
        out_shape=out_shape,
        compiler_params=pltpu.CompilerParams(
            dimension_semantics=("parallel",), vmem_limit_bytes=VMEM_LIMIT_BYTES),
        name="in_proj",
    )(x2d, g, w_main, w_t)


def _fox_decay_kernel(f_ref, bias_ref, cs_ref):
    seq = f_ref.shape[1]
    r = lax.broadcasted_iota(jnp.int32, (LANES, LANES), 0)
    c = lax.broadcasted_iota(jnp.int32, (LANES, LANES), 1)
    lower = (c <= r).astype(BF16)
    place = [((c == r + FOX_HEADS * j) & (r < FOX_HEADS)).astype(BF16) for j in range(3)]
    bias = bias_ref[...]
    carry = jnp.zeros((1, LANES), F32)
    for j in range(seq // LANES):
        rows = slice(j * LANES, (j + 1) * LANES)
        z = f_ref[0, rows, :] + bias
        ls = jnp.minimum(z, 0.0) - jnp.log1p(jnp.exp(-jnp.abs(z)))
        cj = _dot_exact_rhs(lower, ls) + carry
        carry = cj[LANES - 1:LANES, :]
        hi, mid, lo = _split3(cj * (-LOG2E))
        cs_ref[0, rows, :] = (
            _dot(hi, place[0]) + _dot(mid, place[1]) + _dot(lo, place[2])).astype(BF16)


def _fox_decay(f_logits, bias_row):
    batch, seq, _ = f_logits.shape
    return pl.pallas_call(
        _fox_decay_kernel,
        grid=(batch,),
        in_specs=[
            pl.BlockSpec((1, seq, LANES), lambda b: (b, 0, 0)),
            pl.BlockSpec((1, LANES), lambda b: (0, 0)),
        ],
        out_specs=pl.BlockSpec((1, seq, LANES), lambda b: (b, 0, 0)),
        out_shape=jax.ShapeDtypeStruct((batch, seq, LANES), BF16),
        compiler_params=pltpu.CompilerParams(dimension_semantics=("parallel",)),
        name="fox_decay",
    )(f_logits, bias_row)


_HG_LEVELS = (32, 16, 8, 4, 2, 1)


def _hgrn2_decay_matrix():
    c = HG_CHUNK
    t = np.arange(c)[:, None]
    j = np.arange(c)[None, :]
    groups = [(j <= t), (j > t)]
    for n in _HG_LEVELS:
        ref = (t // (2 * n)) * (2 * n) + n - 1
        second = (t & n) != 0
        groups.append(np.where(second, (j > ref) & (j <= t), (j > t) & (j <= ref)))
    return np.concatenate(groups, axis=0).astype(np.float32)


def _hgrn2_kernel(q_ref, f_ref, i_ref, g_ref, lb_ref, og_ref, em_ref, y_ref, st_ref, *, layer):
    ts = q_ref.shape[0]
    c = HG_CHUNK

    @pl.when(pl.program_id(1) == 0)
    def _():
        st_ref[...] = jnp.zeros_like(st_ref)

    lbl = lb_ref[...]
    lbe = jnp.exp(lbl - jnp.max(lbl, axis=0, keepdims=True))
    lb = jnp.sum(lbe[0:layer + 1, :], axis=0, keepdims=True) / jnp.sum(lbe, axis=0, keepdims=True)

    em3 = em_ref[...]
    og = og_ref[...]
    row = lax.broadcasted_iota(jnp.int32, (c, HG_W), 0)
    sublanes = 8

    def pick_side(n, qf, kf):
        if n >= sublanes:
            return jnp.concatenate(
                [(qf if (r0 // n) % 2 else kf)[r0:r0 + n] for r0 in range(0, c, n)], axis=0)
        return jnp.where((row & n) != 0, qf, kf)

    tt = lax.broadcasted_iota(jnp.int32, (c, c), 0)
    ss = lax.broadcasted_iota(jnp.int32, (c, c), 1)
    txs = tt ^ ss
    level_of = [txs >= n for n in reversed(_HG_LEVELS)]
    causal = tt >= ss
    nck = ts // c
    heads = [slice(h * HG_DK, (h + 1) * HG_DK) for h in range(HG_HEADS)]

    qe, kd, e_last, mixed = [], [], [], []
    for ck in range(nck):
        rows = slice(ck * c, (ck + 1) * c)
        qf = _silu(q_ref[rows, :].astype(F32))
        fg = lb + (1.0 - lb) * _sigmoid(f_ref[rows, :])
        kf = 1.0 - fg
        terms = jnp.concatenate(_split3(jnp.log2(fg)), axis=0)
        decay = jnp.exp2(_dot(em3, terms))
        qe.append((qf * decay[0:c]).astype(BF16))
        kd.append((kf * decay[c:2 * c]).astype(BF16))
        e_last.append(decay[c - 1:c, :])
        us = [(qf.astype(BF16), kf.astype(BF16))]
        for li, n in enumerate(_HG_LEVELS):
            u = pick_side(n, qf, kf) * decay[(2 + li) * c:(3 + li) * c]
            us.append((u.astype(BF16),) * 2)
        mixed.append(us)

    scores = []
    for ck in range(nck):
        per_head = []
        for cols in heads:
            qd, kdiag = mixed[ck][0]
            sc = _dot_nt(qd[:, cols], kdiag[:, cols])
            for li in range(len(_HG_LEVELS)):
                u = mixed[ck][len(_HG_LEVELS) - li][0][:, cols]
                sc = jnp.where(level_of[li], _dot_nt(u, u), sc)
            per_head.append(jnp.where(causal, sc, 0.0).astype(BF16))
        scores.append(per_head)

    for ck in range(nck):
        rows = slice(ck * c, (ck + 1) * c)
        v = i_ref[rows, :]
        gate = _silu(g_ref[rows, :].astype(F32))
        for h, cols in enumerate(heads):
            st = st_ref[h]
            o = _dot_nt(qe[ck][:, cols], st.astype(BF16)) + _dot(scores[ck][h], v[:, cols])
            st_ref[h] = st * e_last[ck][:, cols] + _dot_tn(v[:, cols], kd[ck][:, cols])
            o = o * _rms_scale(o) * og
            y_ref[rows, cols] = (o * gate[:, cols]).astype(y_ref.dtype)


def _hgrn2(q, f, i, g, lb_logits, o_gain, layer, batch, seq, ts):
    tokens = q.shape[0]
    per_b = seq // ts
    em = jnp.asarray(np.tile(_hgrn2_decay_matrix(), (1, 3)), dtype=BF16)
    tile = pl.BlockSpec((ts, HG_W), lambda b, s: (b * per_b + s, 0))
    return pl.pallas_call(
        functools.partial(_hgrn2_kernel, layer=layer),
        grid=(batch, per_b),
        in_specs=[
            tile, tile, tile, tile,
            pl.BlockSpec(lb_logits.shape, lambda b, s: (0, 0)),
            pl.BlockSpec((1, HG_DV), lambda b, s: (0, 0)),
            pl.BlockSpec(em.shape, lambda b, s: (0, 0)),
        ],
        out_specs=tile,
        out_shape=jax.ShapeDtypeStruct((tokens, HG_W), BF16),
        scratch_shapes=[pltpu.VMEM((HG_HEADS, HG_DV, HG_DK), F32)],
        compiler_params=pltpu.CompilerParams(dimension_semantics=("parallel", "arbitrary")),
        name="hgrn2",
    )(q, f, i, g, lb_logits, o_gain, em)


_BIAS_TERMS = 3
_PAIRS_PER_STEP = 4


def _pair_norm(x, gain):
    r = lax.broadcasted_iota(jnp.int32, (LANES, LANES), 0)
    c = lax.broadcasted_iota(jnp.int32, (LANES, LANES), 1)
    same_head = (r // FOX_DH == c // FOX_DH).astype(BF16)
    sq = x * x
    hi = sq.astype(BF16)
    lo = (sq - hi.astype(F32)).astype(BF16)
    ssum = _dot(hi, same_head) + _dot(lo, same_head)
    return x * lax.rsqrt(ssum * (1.0 / FOX_DH) + EPS) * gain


def _fox_attn_kernel(qt_ref, k_ref, vt_ref, cs_ref, qg_ref, kg_ref, o_ref,
                     kaug_ref, vaug_ref, acc_ref, s_ref, m_ref, *, tq, tk):
    seq = k_ref.shape[1]
    i = pl.program_id(2)

    @pl.when(i == 0)
    def _():
        kg = kg_ref[...]
        r = lax.broadcasted_iota(jnp.int32, (LANES, LANES), 0)
        c = lax.broadcasted_iota(jnp.int32, (LANES, LANES), 1)
        ones = jnp.ones((FOX_DH, FOX_PREP_ROWS), BF16)
        for u in range(_PAIRS_PER_STEP):
            pair = _PAIRS_PER_STEP * pl.program_id(1) + u
            src = FOX_HEADS * (c % _BIAS_TERMS) + 2 * pair + c // _BIAS_TERMS
            pick = ((r == src) & (c < 2 * _BIAS_TERMS)).astype(BF16)
            f0 = u * LANES
            for r0 in range(0, seq, FOX_PREP_ROWS):
                rows = slice(r0, r0 + FOX_PREP_ROWS)
                kn = _pair_norm(k_ref[0, rows, f0:f0 + LANES].astype(F32), kg)
                kaug_ref[u, rows, 0:LANES] = kn.astype(BF16)
                kaug_ref[u, rows, LANES:2 * LANES] = _dot(cs_ref[0, rows, :], pick).astype(BF16)
                vaug_ref[2 * u, 0:FOX_DH, rows] = vt_ref[0, f0:f0 + FOX_DH, rows]
                vaug_ref[2 * u, FOX_DH:LANES, rows] = ones
                vaug_ref[2 * u + 1, 0:FOX_DH, rows] = ones
                vaug_ref[2 * u + 1, FOX_DH:LANES, rows] = vt_ref[0, f0 + FOX_DH:f0 + LANES, rows]

    def head_norm(xh):
        return xh * lax.rsqrt(jnp.mean(xh * xh, axis=0, keepdims=True) + EPS)

    zeros = jnp.zeros((FOX_DH, tq), F32)
    ri = lax.broadcasted_iota(jnp.int32, (LANES, tq), 0)
    qa = []
    for u in range(_PAIRS_PER_STEP):
        qt = qt_ref[0, u * LANES:(u + 1) * LANES, :].astype(F32)
        qn = jnp.concatenate([head_norm(qt[0:FOX_DH]), head_norm(qt[FOX_DH:LANES])], axis=0)
        qn = qn * qg_ref[...] * (FOX_DH ** -0.5 * LOG2E)
        for hh in range(2):
            feats = [qn[0:FOX_DH], zeros] if hh == 0 else [zeros, qn[FOX_DH:LANES]]
            ind = ((ri >= _BIAS_TERMS * hh) & (ri < _BIAS_TERMS * (hh + 1))).astype(F32)
            qa.append(jnp.concatenate(feats + [ind], axis=0).astype(BF16))
    chains = len(qa)

    acc_ref[...] = jnp.zeros_like(acc_ref)
    on_or_below = (lax.broadcasted_iota(jnp.int32, (tk, tk), 0)
                   <= lax.broadcasted_iota(jnp.int32, (tk, tk), 1))
    tri_bias = jnp.where(on_or_below, 0.0, -jnp.inf).astype(F32)

    m_ref[...] = jnp.full(m_ref.shape, -jnp.inf, F32)

    def put_scores(kb, slot):
        k0 = pl.multiple_of(kb * tk, tk)
        for u in range(_PAIRS_PER_STEP):
            ka = kaug_ref[u, pl.ds(k0, tk), :]
            for hh in range(2):
                s_ref[chains * slot + 2 * u + hh] = _dot(ka, qa[2 * u + hh])

    def update(ch, s, k0, c0):
        cols = slice(c0, c0 + s.shape[1])
        m_old = m_ref[ch, :, cols]
        m_new = jnp.maximum(m_old, jnp.max(s, axis=0, keepdims=True))
        m_ref[ch, :, cols] = m_new
        alpha = jnp.exp2(m_old - m_new)
        pt = jnp.exp2(s - m_new).astype(BF16)
        acc_ref[ch, :, cols] = (
            alpha * acc_ref[ch, :, cols] + _dot(vaug_ref[ch, :, pl.ds(k0, tk)], pt))

    def consume(kb, slot):
        k0 = pl.multiple_of(kb * tk, tk)
        for ch in range(chains):
            update(ch, s_ref[chains * slot + ch], k0, 0)

    def diagonal():
        k0 = pl.multiple_of(2 * i * tk, tk)
        k1 = pl.multiple_of((2 * i + 1) * tk, tk)
        late = []
        for u in range(_PAIRS_PER_STEP):
            ka = kaug_ref[u, pl.ds(k1, tk), :]
            for hh in range(2):
                late.append(_dot(ka, qa[2 * u + hh][:, tk:tq]) + tri_bias)
        for ch in range(chains):
            s = s_ref[ch]
            update(ch, jnp.concatenate([s[:, 0:tk] + tri_bias, s[:, tk:tq]], axis=1), k0, 0)
        for ch in range(chains):
            update(ch, late[ch], k1, tk)

    def two_blocks(j, carry):
        kb = 2 * j
        put_scores(kb + 1, 1)
        consume(kb, 0)
        put_scores(kb + 2, 0)
        consume(kb + 1, 1)
        return carry

    def four_blocks(jj, carry):
        two_blocks(2 * jj, carry)
        return two_blocks(2 * jj + 1, carry)

    put_scores(0, 0)
    lax.fori_loop(0, i // 2, four_blocks, 0)

    @pl.when(i % 2 == 1)
    def _():
        two_blocks(i - 1, 0)

    diagonal()

    for u in range(_PAIRS_PER_STEP):
        a0 = acc_ref[2 * u]
        a1 = acc_ref[2 * u + 1]
        out_t = jnp.concatenate(
            [a0[0:FOX_DH] / a0[FOX_DH:FOX_DH + 1], a1[FOX_DH:LANES] / a1[0:1]], axis=0)
        o_ref[0, :, u * LANES:(u + 1) * LANES] = out_t.T.astype(o_ref.dtype)


def _fox_attn(qt, k, vt, cs, q_gain_cols, k_gain2, tq):
    batch, seq, _ = k.shape
    tk = tq // 2
    width = _PAIRS_PER_STEP * LANES
    chains = 2 * _PAIRS_PER_STEP
    kernel = functools.partial(_fox_attn_kernel, tq=tq, tk=tk)
    return pl.pallas_call(
        kernel,
        grid=(batch, FOX_W // width, seq // tq),
        in_specs=[
            pl.BlockSpec((1, width, tq), lambda b, p, i: (b, p, i)),
            pl.BlockSpec((1, seq, width), lambda b, p, i: (b, 0, p)),
            pl.BlockSpec((1, width, seq), lambda b, p, i: (b, p, 0)),
            pl.BlockSpec((1, seq, LANES), lambda b, p, i: (b, 0, 0)),
            pl.BlockSpec((LANES, tq), lambda b, p, i: (0, 0)),
            pl.BlockSpec((1, LANES), lambda b, p, i: (0, 0)),
        ],
        out_specs=pl.BlockSpec((1, tq, width), lambda b, p, i: (b, i, p)),
        out_shape=jax.ShapeDtypeStruct((batch, seq, FOX_W), BF16),
        scratch_shapes=[
            pltpu.VMEM((_PAIRS_PER_STEP, seq, 2 * LANES), BF16),
            pltpu.VMEM((chains, LANES, seq), BF16),
            pltpu.VMEM((chains, LANES, tq), F32),
            pltpu.VMEM((2 * chains, tk, tq), F32),
            pltpu.VMEM((chains, 1, tq), F32),
        ],
        compiler_params=pltpu.CompilerParams(
            dimension_semantics=("parallel", "parallel", "arbitrary"),
            vmem_limit_bytes=VMEM_LIMIT_BYTES),
        name="fox_attn",
    )(qt, k, vt, cs, q_gain_cols, k_gain2)


_TAIL_ROW_GROUPS = 2


def _ff_chunks(d_ff):
    chunks, off = [], 0
    while off < d_ff:
        wd = min(4 * MXU_WIDTH, d_ff - off)
        chunks.append((off, wd))
        off += wd
    return chunks


def _tail_kernel(x_ref, ya_ref, yb_ref, ga_ref, gb_ref, p_ref, wa_ref, wb_ref, wo_ref,
                 gf_ref, wg_ref, wu_ref, wd_ref, gp_ref, wpg_ref, wpp_ref, o_ref):
    tm = x_ref.shape[0]
    groups = [slice(r0, r0 + tm // _TAIL_ROW_GROUPS) for r0 in range(0, tm, tm // _TAIL_ROW_GROUPS)]

    xs = []
    for rows in groups:
        ga = _sigmoid(ga_ref[rows, :].astype(F32))
        gb = _sigmoid(gb_ref[rows, :].astype(F32))
        merged = (ga * _dot(ya_ref[rows, :], wa_ref[...])
                  + gb * _dot(yb_ref[rows, :], wb_ref[...]))
        xs.append(x_ref[rows, :] + _dot(merged.astype(BF16), wo_ref[...]))

    hfs = [(x * _rms_scale(x) * gf_ref[...]).astype(BF16) for x in xs]
    ffns = [None] * len(groups)
    for off, wd in _ff_chunks(wg_ref.shape[1]):
        for gi, hf in enumerate(hfs):
            act = _silu(_dot(hf, wg_ref[:, off:off + wd])) * _dot(hf, wu_ref[:, off:off + wd])
            part = _dot(act.astype(BF16), wd_ref[off:off + wd, :])
            ffns[gi] = part if ffns[gi] is None else ffns[gi] + part
    xs = [x + ffn for x, ffn in zip(xs, ffns)]

    hps = [(x * _rms_scale(x) * gp_ref[...]).astype(BF16) for x in xs]
    for rows, x, hp in zip(groups, xs, hps):
        emb = _dot(p_ref[rows, :].astype(BF16), wpp_ref[...])
        o_ref[rows, :] = x + _sigmoid(_dot(hp, wpg_ref[...])) * emb


def _tail(x2d, ya, yb, ga, gb, p2d, wa, wb, wo, gf, wg, wu, wd, gp, wpg, wpp, tm):
    tokens, d = x2d.shape

    def tile(width):
        return pl.BlockSpec((tm, width), lambda i: (i, 0))

    def whole(arr):
        return pl.BlockSpec(arr.shape, lambda i: (0, 0), pipeline_mode=pl.Buffered(1))

    return pl.pallas_call(
        _tail_kernel,
        grid=(tokens // tm,),
        in_specs=[
            tile(d), tile(ya.shape[1]), tile(yb.shape[1]), tile(d), tile(d), tile(p2d.shape[1]),
            whole(wa), whole(wb), whole(wo), whole(gf), whole(wg), whole(wu), whole(wd),
            whole(gp), whole(wpg), whole(wpp),
        ],
        out_specs=tile(d),
        out_shape=jax.ShapeDtypeStruct((tokens, d), F32),
        compiler_params=pltpu.CompilerParams(
            dimension_semantics=("parallel",), vmem_limit_bytes=VMEM_LIMIT_BYTES),
        name="tail",
    )(x2d, ya, yb, ga, gb, p2d, wa, wb, wo, gf, wg, wu, wd, gp, wpg, wpp)


def kernel(x, p, norm_mix_g, w_in, hg_lb_logits, hg_onorm_g, fox_f_bias, fox_q_norm_g,
           fox_k_norm_g, w_branch_a, w_branch_b, w_out, norm_ffn_g, w_ffn_gate, w_ffn_up,
           w_ffn_down, norm_ple_g, w_ple_gate, w_ple_proj):
    batch, seq, d = x.shape
    depth = p.shape[0]
    off_fox_q = 4 * HG_W
    off_fox_k = off_fox_q + FOX_W
    off_fox_v = off_fox_k + FOX_W
    off_fox_f = off_fox_v + FOX_W
    off_gate = off_fox_f + FOX_HEADS
    assert w_in.shape[2] == off_gate + 2 * d and d == D_MODEL
    assert seq % TOKEN_TILE == 0 and seq % HG_TILE == 0 and seq % FOX_QUERY_TILE == 0
    assert seq % FOX_PREP_ROWS == 0

    tm = TOKEN_TILE
    ts = HG_TILE
    tq = FOX_QUERY_TILE

    x2d = x.reshape(batch * seq, d)
    for layer in range(depth):
        wl = w_in[layer]
        w_main = jnp.concatenate(
            [wl[:, :off_fox_q], wl[:, off_fox_k:off_fox_v], wl[:, off_gate:],
             jnp.pad(wl[:, off_fox_f:off_gate], ((0, 0), (0, LANES - FOX_HEADS)))],
            axis=1).astype(BF16)
        w_t = jnp.concatenate(
            [wl[:, off_fox_q:off_fox_k], wl[:, off_fox_v:off_fox_f]], axis=1).T.astype(BF16)
        (hg_q, hg_f, hg_i, hg_g, fox_k, gate_a, gate_b, fox_f, fox_qt, fox_vt) = _in_proj(
            x2d, norm_mix_g[layer][None, :], w_main, w_t, batch, seq, tm)

        bias_row = jnp.pad(fox_f_bias[layer].astype(F32), (0, LANES - FOX_HEADS))[None, :]
        cs = _fox_decay(fox_f.reshape(batch, seq, LANES), bias_row)

        y_a = _hgrn2(hg_q, hg_f, hg_i, hg_g, hg_lb_logits.astype(F32),
                     hg_onorm_g[layer].astype(F32)[None, :], layer, batch, seq, ts)

        q_gain_cols = jnp.broadcast_to(
            jnp.tile(fox_q_norm_g[layer].astype(F32), 2)[:, None], (LANES, tq))
        y_b = _fox_attn(fox_qt, fox_k.reshape(batch, seq, FOX_W), fox_vt, cs, q_gain_cols,
                        jnp.tile(fox_k_norm_g[layer].astype(F32), 2)[None, :], tq)

        x2d = _tail(
            x2d, y_a, y_b.reshape(batch * seq, FOX_W), gate_a, gate_b,
            p[layer].reshape(batch * seq, -1),
            w_branch_a[layer].astype(BF16), w_branch_b[layer].astype(BF16),
            w_out[layer].astype(BF16), norm_ffn_g[layer][None, :],
            w_ffn_gate[layer].astype(BF16), w_ffn_up[layer].astype(BF16),
            w_ffn_down[layer].astype(BF16), norm_ple_g[layer][None, :],
            w_ple_gate[layer].astype(BF16), w_ple_proj[layer].astype(BF16), tm)
    return x2d.reshape(batch, seq, d)
```
